```python
import math
import jax, jax.numpy as jnp
from jax import lax
import numpy as np

D_MODEL = 1024
BATCH = 8
SEQ = 2048
DEPTH = 1
DEC_BATCH = 128
DEC_SEQ = 1
PAST_LEN = 16384
PAGE_SIZE = 128

PLE_DIM = 256
A_WIDTH = D_MODEL
A_GROUPS = 8
A_GROUP_DIM = A_WIDTH // A_GROUPS
A_CHUNK = 128
GLA_HEADS = 4
GLA_DK = D_MODEL // 2 // GLA_HEADS
GLA_DV = D_MODEL // GLA_HEADS
GLA_RANK = 16
GLA_TAU = 16.0
GLA_CHUNK = 16
PEER_HEADS = 8
PEER_NKEYS = 128
PEER_N = PEER_NKEYS * PEER_NKEYS
PEER_DKEY = 256
PEER_TOPK = 16
PEER_BLOCK = 128
DEEPNORM_ALPHA = (2.0 * DEPTH) ** 0.25
DEEPNORM_BETA = (8.0 * DEPTH) ** -0.25
LN_EPS = 1e-5
IN_SIZES = (A_WIDTH, A_WIDTH, GLA_HEADS * GLA_DK, GLA_HEADS * GLA_DK, GLA_HEADS * GLA_DV, GLA_HEADS * GLA_DV, GLA_RANK, D_MODEL, D_MODEL)
IN_COLS = sum(IN_SIZES)

kernel_name = 'hybrid_gmlp_gla_peer_decoder_step'


def _layernorm(x, g, b):
    xf = x.astype(jnp.float32)
    mu = jnp.mean(xf, axis=-1, keepdims=True)
    var = jnp.mean(jnp.square(xf - mu), axis=-1, keepdims=True)
    return ((xf - mu) * lax.rsqrt(var + LN_EPS) * g + b).astype(x.dtype)


def _chunk_spatial_gate(u, v, w_s, b_s):
    bsz, L, _ = v.shape
    c = min(L, A_CHUNK)
    n = L // c
    mask = jnp.tril(jnp.ones((c, c), dtype=bool))
    w = jnp.where(mask, w_s[:, :c, :c], 0)
    vc = v.reshape(bsz, n, c, A_GROUPS, A_GROUP_DIM)
    z = jnp.einsum('gts,bnsgd->bntgd', w, vc) + b_s[:, :c].T[:, :, None]
    return u * z.reshape(bsz, L, A_WIDTH)


def _gla_chunk(S, qkvg):
    q, k, v, g = qkvg
    c = q.shape[1]
    b = jnp.cumsum(g, axis=1)
    o_inter = jnp.einsum('bthk,bhkv->bthv', q * jnp.exp(b), S)
    mask = jnp.tril(jnp.ones((c, c), dtype=bool))[None, :, :, None, None]
    diff = b[:, :, None] - b[:, None, :]
    decay = jnp.where(mask, jnp.exp(jnp.minimum(diff, 0.0)), 0.0)
    att = jnp.einsum('bthk,bshk,btshk->bhts', q, k, decay)
    o = o_inter + jnp.einsum('bhts,bshv->bthv', att, v)
    b_last = b[:, -1]
    S_new = jnp.exp(b_last)[..., None] * S + jnp.einsum('bshk,bshv->bhkv', k * jnp.exp(b_last[:, None] - b), v)
    return S_new, o


def _gla(q, k, v, g, S0):
    bsz, L = q.shape[:2]
    c = math.gcd(L, GLA_CHUNK)
    n = L // c

    def to_chunks(t):
        return jnp.moveaxis(t.reshape(bsz, n, c, *t.shape[2:]), 1, 0)

    S_fin, o = lax.scan(_gla_chunk, S0, (to_chunks(q), to_chunks(k), to_chunks(v), to_chunks(g)))
    o = jnp.moveaxis(o, 0, 1).reshape(bsz, L, GLA_HEADS, GLA_DV)
    return o, S_fin


def _peer(x, w_q, keys, u_tab, v_tab):
    bsz, L, d = x.shape
    T = bsz * L
    blk = math.gcd(T, PEER_BLOCK)
    half = PEER_DKEY // 2
    keys_f = keys.astype(jnp.float32)

    def block(xb):
        q = (xb @ w_q).reshape(blk, PEER_HEADS, 2, half).astype(jnp.float32)
        mu = jnp.mean(q, axis=-1, keepdims=True)
        q = (q - mu) * lax.rsqrt(jnp.mean(jnp.square(q - mu), axis=-1, keepdims=True) + LN_EPS)
        s = jnp.einsum('thpd,hpnd->thpn', q, keys_f)
        s1, i1 = lax.top_k(s[:, :, 0], PEER_TOPK)
        s2, i2 = lax.top_k(s[:, :, 1], PEER_TOPK)
        cand = (s1[..., :, None] + s2[..., None, :]).reshape(blk, PEER_HEADS, PEER_TOPK * PEER_TOPK)
        sc, ci = lax.top_k(cand, PEER_TOPK)
        e1 = jnp.take_along_axis(i1, ci // PEER_TOPK, axis=-1)
        e2 = jnp.take_along_axis(i2, ci % PEER_TOPK, axis=-1)
        idx = e1 * PEER_NKEYS + e2
        gate = jax.nn.softmax(sc, axis=-1)
        h = jax.nn.gelu(jnp.einsum('thkd,td->thk', u_tab[idx], xb).astype(jnp.float32))
        w = (gate * h).astype(xb.dtype)
        return jnp.einsum('thk,thkd->td', w, v_tab[idx])

    y = lax.map(block, x.reshape(T // blk, blk, d))
    return y.reshape(bsz, L, d)


def _layer(x, p, S0, w_in, w_s, b_s, ln_v_g, ln_v_b, w_gk, b_gk, gla_norm_g, w_br_a, w_br_b, w_o,
           ln1_g, ln1_b, peer_wq, peer_keys, peer_u, peer_v, ln2_g, ln2_b, w_pe, w_pg):
    bsz, L, _ = x.shape
    proj = x @ w_in
    splits = np.cumsum(IN_SIZES)[:-1].tolist()
    ua, va, q, k, v, r, gk_low, g_a, g_b = jnp.split(proj, splits, axis=-1)
    va_n = _layernorm(jax.nn.gelu(va), ln_v_g, ln_v_b)
    y_a = _chunk_spatial_gate(jax.nn.gelu(ua), va_n, w_s, b_s)
    qf = q.reshape(bsz, L, GLA_HEADS, GLA_DK).astype(jnp.float32) * (GLA_DK ** -0.5)
    kf = k.reshape(bsz, L, GLA_HEADS, GLA_DK).astype(jnp.float32)
    vf = v.reshape(bsz, L, GLA_HEADS, GLA_DV).astype(jnp.float32)
    log_a = jax.nn.log_sigmoid((gk_low @ w_gk + b_gk).astype(jnp.float32)).reshape(bsz, L, GLA_HEADS, GLA_DK) / GLA_TAU
    o, S_new = _gla(qf, kf, vf, log_a, S0.astype(jnp.float32))
    o = o * lax.rsqrt(jnp.mean(jnp.square(o), axis=-1, keepdims=True) + LN_EPS) * gla_norm_g.astype(jnp.float32)
    y_b = jax.nn.silu(r) * o.reshape(bsz, L, GLA_HEADS * GLA_DV).astype(x.dtype)
    m = jax.nn.sigmoid(g_a) * (y_a @ w_br_a) + jax.nn.sigmoid(g_b) * (y_b @ w_br_b)
    x1 = _layernorm(DEEPNORM_ALPHA * x + m @ w_o, ln1_g, ln1_b)
    x2 = _layernorm(DEEPNORM_ALPHA * x1 + _peer(x1, peer_wq, peer_keys, peer_u, peer_v), ln2_g, ln2_b)
    x3 = x2 + (p @ w_pe) * jax.nn.sigmoid(x2 @ w_pg)
    c = min(L, A_CHUNK)
    return x3, S_new, va_n[:, L - c:]


def setup_inputs(seed: int = 0) -> dict:
    key = jax.random.key(seed)
    ks = jax.random.split(key, 32)

    def nrm(k, shape, scale):
        return jax.random.normal(k, shape, jnp.float32) * scale

    return {
        'x_prompt': nrm(ks[0], (BATCH, SEQ, D_MODEL), 1.0),
        'x_sample': nrm(ks[1], (DEC_BATCH, DEC_SEQ, D_MODEL), 1.0),
        'state_gla': nrm(ks[2], (DEPTH, DEC_BATCH, GLA_HEADS, GLA_DK, GLA_DV), 0.5),
        'p_prompt': nrm(ks[3], (DEPTH, BATCH, SEQ, PLE_DIM), 1.0),
        'p_sample': nrm(ks[4], (DEPTH, DEC_BATCH, DEC_SEQ, PLE_DIM), 1.0),
        'w_in': nrm(ks[5], (DEPTH, D_MODEL, IN_COLS), D_MODEL ** -0.5),
        'w_s': nrm(ks[6], (DEPTH, A_GROUPS, A_CHUNK, A_CHUNK), A_CHUNK ** -0.5),
        'b_s': 1.0 + nrm(ks[7], (DEPTH, A_GROUPS, A_CHUNK), 0.1),
        'ln_v_g': 1.0 + nrm(ks[8], (DEPTH, A_WIDTH), 0.05),
        'ln_v_b': nrm(ks[9], (DEPTH, A_WIDTH), 0.02),
        'w_gk': nrm(ks[10], (DEPTH, GLA_RANK, GLA_HEADS * GLA_DK), GLA_RANK ** -0.5),
        'b_gk': nrm(ks[11], (DEPTH, GLA_HEADS * GLA_DK), 0.1),
        'gla_norm_g': 1.0 + nrm(ks[12], (DEPTH, GLA_DV), 0.05),
        'w_br_a': nrm(ks[13], (DEPTH, A_WIDTH, D_MODEL), A_WIDTH ** -0.5),
        'w_br_b': nrm(ks[14], (DEPTH, GLA_HEADS * GLA_DV, D_MODEL), (GLA_HEADS * GLA_DV) ** -0.5),
        'w_o': nrm(ks[15], (DEPTH, D_MODEL, D_MODEL), DEEPNORM_BETA * D_MODEL ** -0.5),
        'ln1_g': 1.0 + nrm(ks[16], (DEPTH, D_MODEL), 0.05),
        'ln1_b': nrm(ks[17], (DEPTH, D_MODEL), 0.02),
        'peer_wq': nrm(ks[18], (DEPTH, D_MODEL, PEER_HEADS * PEER_DKEY), D_MODEL ** -0.5),
        'peer_keys': nrm(ks[19], (DEPTH, PEER_HEADS, 2, PEER_NKEYS, PEER_DKEY // 2), (PEER_DKEY // 2) ** -0.5),
        'peer_u': nrm(ks[20], (DEPTH, PEER_N, D_MODEL), D_MODEL ** -0.5),
        'peer_v': nrm(ks[21], (DEPTH, PEER_N, D_MODEL), DEEPNORM_BETA),
        'ln2_g': 1.0 + nrm(ks[22], (DEPTH, D_MODEL), 0.05),
        'ln2_b': nrm(ks[23], (DEPTH, D_MODEL), 0.02),
        'w_pe': nrm(ks[24], (DEPTH, PLE_DIM, D_MODEL), PLE_DIM ** -0.5),
        'w_pg': nrm(ks[25], (DEPTH, D_MODEL, D_MODEL), D_MODEL ** -0.5),
    }


def reference(x_prompt, x_sample, state_gla, p_prompt, p_sample, w_in, w_s, b_s, ln_v_g, ln_v_b, w_gk, b_gk,
              gla_norm_g, w_br_a, w_br_b, w_o, ln1_g, ln1_b, peer_wq, peer_keys, peer_u, peer_v, ln2_g, ln2_b,
              w_pe, w_pg):
    yp, ys = x_prompt, x_sample
    gla_p, gla_s, cv_p, cv_s = [], [], [], []
    for i in range(DEPTH):
        params = (w_in[i], w_s[i], b_s[i], ln_v_g[i], ln_v_b[i], w_gk[i], b_gk[i], gla_norm_g[i], w_br_a[i],
                  w_br_b[i], w_o[i], ln1_g[i], ln1_b[i], peer_wq[i], peer_keys[i], peer_u[i], peer_v[i],
                  ln2_g[i], ln2_b[i], w_pe[i], w_pg[i])
        s0_prompt = jnp.zeros((x_prompt.shape[0], GLA_HEADS, GLA_DK, GLA_DV), jnp.float32)
        yp, sp, vp = _layer(yp, p_prompt[i], s0_prompt, *params)
        ys, ss, vs = _layer(ys, p_sample[i], state_gla[i], *params)
        gla_p.append(sp)
        gla_s.append(ss)
        cv_p.append(vp)
        cv_s.append(vs)
    return (yp, ys, jnp.stack(gla_p), jnp.stack(gla_s), jnp.stack(cv_p), jnp.stack(cv_s))
```

```python
import functools
import math

import jax
import jax.numpy as jnp
from jax import lax
from jax.experimental import pallas as pl
from jax.experimental.pallas import tpu as pltpu

F32 = jnp.float32
BF16 = jnp.bfloat16

LANES = 128
A_CHUNK = 128
A_GROUPS = 8
GLA_HEADS = 4
GLA_TAU = 16.0
GLA_CHUNK = 128
PEER_HEADS = 8
PEER_NKEYS = 128
PEER_TOPK = 16
LN_EPS = 1e-5
NOT_SELECTED = 255.0
VMEM_LIMIT = 56 * 1024 * 1024


def _dot(a, b):
    return jnp.dot(a, b, preferred_element_type=F32)


def _dot_nt(a, b):
    return lax.dot_general(a, b, (((1,), (1,)), ((), ())), preferred_element_type=F32)


def _layernorm(x, g, b):
    mu = jnp.mean(x, axis=-1, keepdims=True)
    xc = x - mu
    var = jnp.mean(xc * xc, axis=-1, keepdims=True)
    return xc * lax.rsqrt(var + LN_EPS) * g + b


def _log_sigmoid(z):
    return jnp.minimum(z, 0.0) - jnp.log1p(jnp.exp(-jnp.abs(z)))


def _const_spec(shape):
    nd = len(shape)
    return pl.BlockSpec(shape, lambda *_: (0,) * nd, pipeline_mode=pl.Buffered(1))


def _params(semantics):
    return pltpu.CompilerParams(dimension_semantics=semantics, vmem_limit_bytes=VMEM_LIMIT)


def _stage1_kernel(x_ref, wua_ref, wva_ref, wq_ref, wk_ref, wv_ref, wr_ref, wgl_ref, wgk_ref, bgk_ref,
                   wga_ref, wgb_ref, ws_ref, bs_ref, lng_ref, lnb_ref, wbra_ref,
                   ma_ref, q_ref, k_ref, v_ref, b_ref, sr_ref, sgb_ref, cv_ref, ya_ref, *, decode, tm, dk):
    xb = x_ref[...].astype(BF16)
    gu = jax.nn.gelu(_dot(xb, wua_ref[...]))
    van = _layernorm(jax.nn.gelu(_dot(xb, wva_ref[...])), lng_ref[...], lnb_ref[...])
    if decode:
        cv_ref[...] = van
        ya_ref[...] = (gu * (van * ws_ref[...] + bs_ref[...])).astype(BF16)
    else:
        cv_ref[0] = van[tm - A_CHUNK:, :]
        vb = van.astype(BF16)
        row = lax.broadcasted_iota(jnp.int32, (A_CHUNK, A_CHUNK), 0)
        col = lax.broadcasted_iota(jnp.int32, (A_CHUNK, A_CHUNK), 1)
        for g in range(A_GROUPS):
            w = jnp.where(row >= col, ws_ref[g], 0.0).astype(BF16)
            cs = slice(g * LANES, (g + 1) * LANES)
            for c in range(tm // A_CHUNK):
                rs = slice(c * A_CHUNK, (c + 1) * A_CHUNK)
                z = _dot(w, vb[rs, cs]) + bs_ref[:, cs]
                ya_ref[rs, cs] = (gu[rs, cs] * z).astype(BF16)
    ma_ref[...] = jax.nn.sigmoid(_dot(xb, wga_ref[...])) * _dot(ya_ref[...], wbra_ref[...])
    q_ref[...] = _dot(xb, wq_ref[...]) * (dk ** -0.5)
    k_ref[...] = _dot(xb, wk_ref[...])
    v_ref[...] = _dot(xb, wv_ref[...])
    sr_ref[...] = jax.nn.silu(_dot(xb, wr_ref[...]))
    sgb_ref[...] = jax.nn.sigmoid(_dot(xb, wgb_ref[...]))
    gk_low = _dot(xb, wgl_ref[...]).astype(BF16)
    log_a = _log_sigmoid(_dot(gk_low, wgk_ref[...]) + bgk_ref[...]) / GLA_TAU
    if not decode:
        pos = lax.broadcasted_iota(jnp.int32, log_a.shape, 0) % GLA_CHUNK
        shift = 1
        while shift < GLA_CHUNK:
            log_a = log_a + jnp.where(pos >= shift, pltpu.roll(log_a, shift, axis=0), 0.0)
            shift *= 2
    b_ref[...] = log_a


def _stage1(x, w, *, decode, tm, seq_len):
    t, d = x.shape
    dqk = w['wq'].shape[1]
    dv = w['wv'].shape[1]
    steps = t // tm
    row = lambda n: pl.BlockSpec((tm, n), lambda i: (i, 0))
    if decode:
        cv_shape = jax.ShapeDtypeStruct((t, d), F32)
        cv_spec = row(d)
        ws, bs = w['ws00'], w['bs0']
    else:
        per_seq = seq_len // tm
        cv_shape = jax.ShapeDtypeStruct((t // seq_len, A_CHUNK, d), F32)
        cv_spec = pl.BlockSpec((1, A_CHUNK, d), lambda i: (i // per_seq, 0, 0))
        ws, bs = w['ws'], w['bsx']
    weights = [w['wua'], w['wva'], w['wq'], w['wk'], w['wv'], w['wr'], w['wgl'], w['wgk'], w['bgk'],
               w['wga'], w['wgb'], ws, bs, w['lng'], w['lnb'], w['wbra']]
    out_shape = [jax.ShapeDtypeStruct((t, d), F32),
                 jax.ShapeDtypeStruct((t, dqk), F32),
                 jax.ShapeDtypeStruct((t, dqk), F32),
                 jax.ShapeDtypeStruct((t, dv), F32),
                 jax.ShapeDtypeStruct((t, dqk), F32),
                 jax.ShapeDtypeStruct((t, dv), F32),
                 jax.ShapeDtypeStruct((t, d), F32),
                 cv_shape]
    out_specs = [row(d), row(dqk), row(dqk), row(dv), row(dqk), row(dv), row(d), cv_spec]
    return pl.pallas_call(
        functools.partial(_stage1_kernel, decode=decode, tm=tm, dk=dqk // GLA_HEADS),
        grid=(steps,),
        in_specs=[row(d)] + [_const_spec(a.shape) for a in weights],
        out_specs=out_specs,
        out_shape=out_shape,
        scratch_shapes=[pltpu.VMEM((tm, d), BF16)],
        compiler_params=_params(("arbitrary",)),
        name="stage1_decode" if decode else "stage1_prompt",
    )(x, *weights)


def _block_ref_rows(b, w):
    c, n = b.shape
    if w >= 8:
        r = b.reshape(c // (2 * w), 2 * w, n)
        return jnp.broadcast_to(r[:, w - 1:w, :], r.shape).reshape(c, n)
    r = b.reshape(c // 8, 8, n)
    sub = lax.broadcasted_iota(jnp.int32, r.shape, 1)
    pick = lambda j: jnp.broadcast_to(r[:, j:j + 1, :], r.shape)
    if w == 4:
        out = pick(3)
    elif w == 2:
        out = jnp.where(sub < 4, pick(1), pick(5))
    else:
        out = jnp.where(sub < 2, pick(0), jnp.where(sub < 4, pick(2), jnp.where(sub < 6, pick(4), pick(6))))
    return out.reshape(c, n)


def _gla_kernel(q_ref, k_ref, b_ref, v_ref, sr_ref, gam_ref, yb_ref, sfin_ref, s_ref, *, dk, dv):
    c = pl.program_id(1)

    @pl.when(c == 0)
    def _():
        s_ref[...] = jnp.zeros_like(s_ref)

    cc = GLA_CHUNK
    row = lax.broadcasted_iota(jnp.int32, (cc, cc), 0)
    col = lax.broadcasted_iota(jnp.int32, (cc, cc), 1)
    rowk = lax.broadcasted_iota(jnp.int32, (cc, dk), 0)
    for h in range(GLA_HEADS):
        ks = slice(h * dk, (h + 1) * dk)
        vs = slice(h * dv, (h + 1) * dv)
        qh, kh, bh = q_ref[:, ks], k_ref[:, ks], b_ref[:, ks]
        vb = v_ref[:, vs].astype(BF16)
        s_old = s_ref[h]
        o = _dot((qh * jnp.exp(bh)).astype(BF16), s_old.astype(BF16))
        att = jnp.where(row == col, jnp.sum(qh * kh, axis=1, keepdims=True), 0.0)
        w = cc // 2
        while w >= 1:
            bref = _block_ref_rows(bh, w)
            second = (rowk // w) % 2 == 1
            ql = jnp.where(second, qh * jnp.exp(jnp.minimum(bh - bref, 0.0)), 0.0)
            kl = jnp.where(second, 0.0, kh * jnp.exp(jnp.minimum(bref - bh, 0.0)))
            a_l = _dot_nt(ql.astype(BF16), kl.astype(BF16))
            att = att + jnp.where(row // (2 * w) == col // (2 * w), a_l, 0.0)
            w //= 2
        o = o + _dot(att.astype(BF16), vb)
        b_last = bh[cc - 1:cc, :]
        kdec_t = (kh * jnp.exp(b_last - bh)).T.astype(BF16)
        decay_col = jnp.exp(bh.T[:, cc - 1:cc])
        s_new = decay_col * s_old + _dot(kdec_t, vb)
        s_ref[h] = s_new
        o = o * lax.rsqrt(jnp.mean(o * o, axis=-1, keepdims=True) + LN_EPS) * gam_ref[...]
        yb_ref[:, vs] = (sr_ref[:, vs] * o).astype(BF16)

    @pl.when(c == pl.num_programs(1) - 1)
    def _():
        sfin_ref[0] = s_ref[...]


def _gla_prompt(q, k, b, v, sr, gamma, *, bsz, seq_len):
    t, dqk = q.shape
    dvt = v.shape[1]
    dk, dv = dqk // GLA_HEADS, dvt // GLA_HEADS
    nc = seq_len // GLA_CHUNK
    row = lambda n: pl.BlockSpec((GLA_CHUNK, n), lambda i, c: (i * nc + c, 0))
    return pl.pallas_call(
        functools.partial(_gla_kernel, dk=dk, dv=dv),
        grid=(bsz, nc),
        in_specs=[row(dqk), row(dqk), row(dqk), row(dvt), row(dvt), _const_spec(gamma.shape)],
        out_specs=[row(dvt), pl.BlockSpec((1, GLA_HEADS, dk, dv), lambda i, c: (i, 0, 0, 0))],
        out_shape=[jax.ShapeDtypeStruct((t, dvt), BF16),
                   jax.ShapeDtypeStruct((bsz, GLA_HEADS, dk, dv), F32)],
        scratch_shapes=[pltpu.VMEM((GLA_HEADS, dk, dv), F32)],
        compiler_params=_params(("arbitrary", "arbitrary")),
        name="gla_prompt",
    )(q, k, b, v, sr, gamma)


DEC_TOKENS = 8


def _gla_step_kernel(qt_ref, kt_ref, gt_ref, v_ref, sr_ref, gam_ref, s_ref, yb_ref, sout_ref, *, dk, dv):
    for j in range(DEC_TOKENS):
        for h in range(GLA_HEADS):
            ks = slice(h * dk, (h + 1) * dk)
            vs = slice(h * dv, (h + 1) * dv)
            decay = jnp.exp(gt_ref[0, ks, j:j + 1])
            s_new = decay * s_ref[j, h] + kt_ref[0, ks, j:j + 1] * v_ref[j:j + 1, vs]
            sout_ref[j, h] = s_new
            o = jnp.sum(qt_ref[0, ks, j:j + 1] * s_new, axis=0, keepdims=True)
            o = o * lax.rsqrt(jnp.mean(o * o, axis=-1, keepdims=True) + LN_EPS) * gam_ref[...]
            yb_ref[j:j + 1, vs] = (sr_ref[j:j + 1, vs] * o).astype(BF16)


def _gla_step(q, k, g, v, sr, gamma, state):
    t, dqk = q.shape
    dvt = v.shape[1]
    dk, dv = dqk // GLA_HEADS, dvt // GLA_HEADS
    steps = t // DEC_TOKENS
    tr = lambda a: a.reshape(steps, DEC_TOKENS, dqk).transpose(0, 2, 1)
    col = pl.BlockSpec((1, dqk, DEC_TOKENS), lambda i: (i, 0, 0))
    row = lambda n: pl.BlockSpec((DEC_TOKENS, n), lambda i: (i, 0))
    st = pl.BlockSpec((DEC_TOKENS, GLA_HEADS, dk, dv), lambda i: (i, 0, 0, 0))
    return pl.pallas_call(
        functools.partial(_gla_step_kernel, dk=dk, dv=dv),
        grid=(steps,),
        in_specs=[col, col, col, row(dvt), row(dvt), _const_spec(gamma.shape), st],
        out_specs=[row(dvt), st],
        out_shape=[jax.ShapeDtypeStruct((t, dvt), BF16), jax.ShapeDtypeStruct(state.shape, F32)],
        compiler_params=_params(("arbitrary",)),
        name="gla_step",
    )(tr(q), tr(k), tr(g), v, sr, gamma, state)


def _merge_kernel(ma_ref, sgb_ref, yb_ref, x_ref, wbrb_ref, wo_ref, g_ref, b_ref, x1_ref, x1t_ref, *, alpha):
    m = ma_ref[...] + sgb_ref[...] * _dot(yb_ref[...], wbrb_ref[...])
    x1 = _layernorm(alpha * x_ref[...] + _dot(m.astype(BF16), wo_ref[...]), g_ref[...], b_ref[...])
    x1_ref[...] = x1
    x1t_ref[...] = x1.T.astype(BF16)


def _merge(ma, sgb, yb, x, w, *, tm, alpha):
    t, d = x.shape
    row = pl.BlockSpec((tm, d), lambda i: (i, 0))
    weights = [w['wbrb'], w['wo'], w['ln1g'], w['ln1b']]
    return pl.pallas_call(
        functools.partial(_merge_kernel, alpha=alpha),
        grid=(t // tm,),
        in_specs=[row, row, row, row] + [_const_spec(a.shape) for a in weights],
        out_specs=[row, pl.BlockSpec((d, tm), lambda i: (0, i))],
        out_shape=[jax.ShapeDtypeStruct((t, d), F32), jax.ShapeDtypeStruct((d, t), BF16)],
        compiler_params=_params(("arbitrary",)),
        name="merge_ln1",
    )(ma, sgb, yb, x, *weights)


def _top_k_rows(s, iota, vals_ref):
    n = s.shape[0]
    rank = jnp.full(s.shape, NOT_SELECTED, F32)
    for r in range(PEER_TOPK):
        m = jnp.max(s, axis=0, keepdims=True)
        first = jnp.min(jnp.where(s == m, iota, float(n)), axis=0, keepdims=True)
        sel = iota == first
        rank = jnp.where(sel, float(r), rank)
        vals_ref[r:r + 1, :] = m
        s = jnp.where(sel, -jnp.inf, s)
    return rank


def _route_kernel(x1_ref, wq_ref, keys_ref, rank2_ref, e2_ref, cnt_ref, c_ref, v1_ref, v2_ref, cand_ref):
    tm = x1_ref.shape[0]
    kk = PEER_TOPK
    q = _dot(x1_ref[...].astype(BF16), wq_ref[...])
    iota_n = lax.broadcasted_iota(jnp.int32, (PEER_NKEYS, tm), 0).astype(F32)
    iota_c = lax.broadcasted_iota(jnp.int32, (kk * kk, tm), 0).astype(F32)
    for h in range(PEER_HEADS):
        scores = []
        for p in range(2):
            hp = 2 * h + p
            qq = q[:, hp * LANES:(hp + 1) * LANES]
            mu = jnp.mean(qq, axis=-1, keepdims=True)
            qc = qq - mu
            qn = qc * lax.rsqrt(jnp.mean(qc * qc, axis=-1, keepdims=True) + LN_EPS)
            scores.append(_dot_nt(keys_ref[hp], qn.astype(BF16)))
        s1, s2 = scores
        rank1 = _top_k_rows(s1, iota_n, v1_ref)
        rank2 = _top_k_rows(s2, iota_n, v2_ref)
        for r1 in range(kk):
            cand_ref[r1 * kk:(r1 + 1) * kk, :] = v1_ref[r1:r1 + 1, :] + v2_ref[...]
        cand = cand_ref[...]
        best = v1_ref[0:1, :] + v2_ref[0:1, :]
        chosen = jnp.zeros(cand.shape, F32)
        z = jnp.zeros((1, tm), F32)
        for r in range(kk):
            m = jnp.max(cand, axis=0, keepdims=True)
            first = jnp.min(jnp.where(cand == m, iota_c, float(kk * kk)), axis=0, keepdims=True)
            sel = iota_c == first
            chosen = jnp.where(sel, 1.0, chosen)
            z = z + jnp.exp(m - best)
            cand = jnp.where(sel, -jnp.inf, cand)
        cnt = jnp.zeros((PEER_NKEYS, tm), F32)
        for r1 in range(kk):
            count = jnp.sum(chosen[r1 * kk:(r1 + 1) * kk, :], axis=0, keepdims=True)
            cnt = cnt + jnp.where(rank1 == float(r1), count, 0.0)
        rank2_ref[h] = rank2
        cnt_ref[h] = cnt
        c_ref[h] = jnp.exp(s1 - v1_ref[0:1, :]) / z
        e2_ref[h] = jnp.exp(s2 - v2_ref[0:1, :])


def _route(x1, wq, keys, *, tm):
    t, d = x1.shape
    fac = pl.BlockSpec((PEER_HEADS, PEER_NKEYS, tm), lambda i: (0, 0, i))
    fac_shape = jax.ShapeDtypeStruct((PEER_HEADS, PEER_NKEYS, t), F32)
    return pl.pallas_call(
        _route_kernel,
        grid=(t // tm,),
        in_specs=[pl.BlockSpec((tm, d), lambda i: (i, 0)), _const_spec(wq.shape), _const_spec(keys.shape)],
        out_specs=[fac, fac, fac, fac],
        out_shape=[fac_shape] * 4,
        scratch_shapes=[pltpu.VMEM((PEER_TOPK, tm), F32), pltpu.VMEM((PEER_TOPK, tm), F32),
                        pltpu.VMEM((PEER_TOPK * PEER_TOPK, tm), F32)],
        compiler_params=_params(("arbitrary",)),
        name="peer_route",
    )(x1, wq, keys)


def _experts_kernel(x1t_ref, u_ref, vt_ref, rank2_ref, e2_ref, cnt_ref, c_ref, o_ref, acc_ref, wt_ref, *, na):
    j = pl.program_id(1)

    @pl.when(j == 0)
    def _():
        acc_ref[...] = jnp.zeros_like(acc_ref)

    ht = jax.nn.gelu(_dot(u_ref[...], x1t_ref[...]))
    for a in range(na):
        gate = None
        for h in range(PEER_HEADS):
            cnt_row = cnt_ref[h, pl.ds(j * na + a, 1), :]
            c_row = c_ref[h, pl.ds(j * na + a, 1), :]
            term = jnp.where(rank2_ref[h] < cnt_row, e2_ref[h] * c_row, 0.0)
            gate = term if gate is None else gate + term
        rs = slice(a * PEER_NKEYS, (a + 1) * PEER_NKEYS)
        wt_ref[rs, :] = (ht[rs, :] * gate).astype(BF16)
    acc_ref[...] += _dot(vt_ref[...], wt_ref[...])

    @pl.when(j == pl.num_programs(1) - 1)
    def _():
        o_ref[...] = acc_ref[...].T


def _experts(x1t, u, vt, rank2, e2, cnt, c, *, tm, tn):
    d, t = x1t.shape
    n = u.shape[0]
    na = tn // PEER_NKEYS
    fac = pl.BlockSpec((PEER_HEADS, PEER_NKEYS, tm), lambda i, j: (0, 0, i))
    return pl.pallas_call(
        functools.partial(_experts_kernel, na=na),
        grid=(t // tm, n // tn),
        in_specs=[pl.BlockSpec((d, tm), lambda i, j: (0, i)),
                  pl.BlockSpec((tn, d), lambda i, j: (j, 0)),
                  pl.BlockSpec((d, tn), lambda i, j: (0, j)),
                  fac, fac, fac, fac],
        out_specs=pl.BlockSpec((tm, d), lambda i, j: (i, 0)),
        out_shape=jax.ShapeDtypeStruct((t, d), F32),
        scratch_shapes=[pltpu.VMEM((d, tm), F32), pltpu.VMEM((tn, tm), BF16)],
        compiler_params=_params(("arbitrary", "arbitrary")),
        name="peer_experts",
    )(x1t, u, vt, rank2, e2, cnt, c)


def _final_kernel(x1_ref, peer_ref, p_ref, g_ref, b_ref, wpe_ref, wpg_ref, o_ref, *, alpha):
    x2 = _layernorm(alpha * x1_ref[...] + peer_ref[...], g_ref[...], b_ref[...])
    emb = _dot(p_ref[...].astype(BF16), wpe_ref[...])
    o_ref[...] = x2 + emb * jax.nn.sigmoid(_dot(x2.astype(BF16), wpg_ref[...]))


def _final(x1, peer, p, w, *, tm, alpha):
    t, d = x1.shape
    row = pl.BlockSpec((tm, d), lambda i: (i, 0))
    weights = [w['ln2g'], w['ln2b'], w['wpe'], w['wpg']]
    return pl.pallas_call(
        functools.partial(_final_kernel, alpha=alpha),
        grid=(t // tm,),
        in_specs=[row, row, pl.BlockSpec((tm, p.shape[1]), lambda i: (i, 0))]
                 + [_const_spec(a.shape) for a in weights],
        out_specs=row,
        out_shape=jax.ShapeDtypeStruct((t, d), F32),
        compiler_params=_params(("arbitrary",)),
        name="ln2_embed",
    )(x1, peer, p, *weights)


def _prep_weights(w_in, w_s, b_s, ln_v_g, ln_v_b, w_gk, b_gk, gla_norm_g, w_br_a, w_br_b, w_o, ln1_g, ln1_b,
                  peer_wq, peer_keys, peer_u, peer_v, ln2_g, ln2_b, w_pe, w_pg):
    d = w_in.shape[0]
    aw = ln_v_g.shape[0]
    dqk = w_gk.shape[1]
    dvt = w_br_b.shape[0]
    rank = w_gk.shape[0]
    sizes = (aw, aw, dqk, dqk, dvt, dvt, rank, d, d)
    offs = [0]
    for s in sizes:
        offs.append(offs[-1] + s)
    cols = lambda i: w_in[:, offs[i]:offs[i + 1]].astype(BF16)
    r2 = lambda a: a.reshape(1, -1)
    gdim = aw // A_GROUPS
    nk = peer_keys.shape[2]
    return {
        'wua': cols(0), 'wva': cols(1), 'wq': cols(2), 'wk': cols(3), 'wv': cols(4), 'wr': cols(5),
        'wgl': jnp.pad(cols(6), ((0, 0), (0, LANES - rank))),
        'wgk': jnp.pad(w_gk.astype(BF16), ((0, LANES - rank), (0, 0))),
        'bgk': r2(b_gk), 'wga': cols(7), 'wgb': cols(8),
        'ws': w_s,
        'bsx': jnp.repeat(b_s.T, gdim, axis=1),
        'ws00': r2(jnp.repeat(w_s[:, 0, 0], gdim)),
        'bs0': r2(jnp.repeat(b_s[:, 0], gdim)),
        'lng': r2(ln_v_g), 'lnb': r2(ln_v_b),
        'wbra': w_br_a.astype(BF16), 'wbrb': w_br_b.astype(BF16), 'wo': w_o.astype(BF16),
        'gamma': r2(gla_norm_g), 'ln1g': r2(ln1_g), 'ln1b': r2(ln1_b),
        'pwq': peer_wq.astype(BF16),
        'keys': peer_keys.astype(BF16).reshape(-1, nk, peer_keys.shape[3]),
        'pu': peer_u.astype(BF16), 'pvt': peer_v.astype(BF16).T,
        'ln2g': r2(ln2_g), 'ln2b': r2(ln2_b), 'wpe': w_pe.astype(BF16), 'wpg': w_pg.astype(BF16),
    }


def _block(t, cap):
    best = LANES
    for m in range(LANES, cap + 1, LANES):
        if t % m == 0:
            best = m
    return best


def _layer(x, p, state, w, *, alpha):
    bsz, seq_len, d = x.shape
    t = bsz * seq_len
    x2d = x.reshape(t, d)
    decode = seq_len == 1
    assert decode or (seq_len % A_CHUNK == 0 and state is None)
    assert t % LANES == 0
    if decode:
        ma, q, k, v, b, sr, sgb, cv = _stage1(x2d, w, decode=True, tm=_block(t, 256), seq_len=1)
        yb, s_new = _gla_step(q, k, b, v, sr, w['gamma'], state)
        cv = cv.reshape(bsz, 1, d)
    else:
        ma, q, k, v, b, sr, sgb, cv = _stage1(x2d, w, decode=False, tm=_block(seq_len, 256), seq_len=seq_len)
        yb, s_new = _gla_prompt(q, k, b, v, sr, w['gamma'], bsz=bsz, seq_len=seq_len)
    x1, x1t = _merge(ma, sgb, yb, x2d, w, tm=_block(t, 512), alpha=alpha)
    rank2, e2, cnt, c = _route(x1, w['pwq'], w['keys'], tm=LANES)
    peer = _experts(x1t, w['pu'], w['pvt'], rank2, e2, cnt, c, tm=_block(t, 512), tn=512)
    x3 = _final(x1, peer, p.reshape(t, -1), w, tm=_block(t, 512), alpha=alpha)
    return x3.reshape(bsz, seq_len, d), s_new, cv


def kernel(x_prompt, x_sample, state_gla, p_prompt, p_sample, w_in, w_s, b_s, ln_v_g, ln_v_b, w_gk, b_gk,
           gla_norm_g, w_br_a, w_br_b, w_o, ln1_g, ln1_b, peer_wq, peer_keys, peer_u, peer_v, ln2_g, ln2_b,
           w_pe, w_pg):
    depth = w_in.shape[0]
    alpha = (2.0 * depth) ** 0.25
    yp, ys = x_prompt, x_sample
    gla_p, gla_s, cv_p, cv_s = [], [], [], []
    for i in range(depth):
        w = _prep_weights(w_in[i], w_s[i], b_s[i], ln_v_g[i], ln_v_b[i], w_gk[i], b_gk[i], gla_norm_g[i],
                          w_br_a[i], w_br_b[i], w_o[i], ln1_g[i], ln1_b[i], peer_wq[i], peer_keys[i],
                          peer_u[i], peer_v[i], ln2_g[i], ln2_b[i], w_pe[i], w_pg[i])
        yp, sp, vp = _layer(yp, p_prompt[i], None, w, alpha=alpha)
        ys, ss, vs = _layer(ys, p_sample[i], state_gla[i], w, alpha=alpha)
        gla_p.append(sp)
        gla_s.append(ss)
        cv_p.append(vp)
        cv_s.append(vs)
    return (yp, ys, jnp.stack(gla_p), jnp.stack(gla_s), jnp.stack(cv_p), jnp.stack(cv_s))
```

```python
import functools
import math

import jax
import jax.numpy as jnp
from jax import lax
from jax.experimental import pallas as pl
from jax.experimental.pallas import tpu as pltpu

F32 = jnp.float32
BF16 = jnp.bfloat16

LANES = 128
A_CHUNK = 128
A_GROUPS = 8
GLA_HEADS = 4
GLA_TAU = 16.0
GLA_CHUNK = 128
PEER_HEADS = 8
PEER_NKEYS = 128
PEER_TOPK = 16
LN_EPS = 1e-5
NOT_SELECTED = 255.0
VMEM_LIMIT = 56 * 1024 * 1024


def _dot(a, b):
    return jnp.dot(a, b, preferred_element_type=F32)


def _dot_nt(a, b):
    return lax.dot_general(a, b, (((1,), (1,)), ((), ())), preferred_element_type=F32)


def _layernorm(x, g, b):
    mu = jnp.mean(x, axis=-1, keepdims=True)
    xc = x - mu
    var = jnp.mean(xc * xc, axis=-1, keepdims=True)
    return xc * lax.rsqrt(var + LN_EPS) * g + b


def _log_sigmoid(z):
    return jnp.minimum(z, 0.0) - jnp.log1p(jnp.exp(-jnp.abs(z)))


def _const_spec(shape):
    nd = len(shape)
    return pl.BlockSpec(shape, lambda *_: (0,) * nd, pipeline_mode=pl.Buffered(1))


def _params(semantics):
    return pltpu.CompilerParams(dimension_semantics=semantics, vmem_limit_bytes=VMEM_LIMIT)


def _stage1_kernel(x_ref, wua_ref, wva_ref, wq_ref, wk_ref, wv_ref, wr_ref, wgl_ref, wgk_ref, bgk_ref,
                   wga_ref, wgb_ref, ws_ref, bs_ref, lng_ref, lnb_ref, wbra_ref,
                   ma_ref, q_ref, k_ref, v_ref, b_ref, sr_ref, sgb_ref, cv_ref, ya_ref, *, decode, tm, dk):
    xb = x_ref[...].astype(BF16)
    gu = jax.nn.gelu(_dot(xb, wua_ref[...]))
    van = _layernorm(jax.nn.gelu(_dot(xb, wva_ref[...])), lng_ref[...], lnb_ref[...])
    if decode:
        cv_ref[...] = van
        ya_ref[...] = (gu * (van * ws_ref[...] + bs_ref[...])).astype(BF16)
    else:
        cv_ref[0] = van[tm - A_CHUNK:, :]
        vb = van.astype(BF16)
        row = lax.broadcasted_iota(jnp.int32, (A_CHUNK, A_CHUNK), 0)
        col = lax.broadcasted_iota(jnp.int32, (A_CHUNK, A_CHUNK), 1)
        for g in range(A_GROUPS):
            w = jnp.where(row >= col, ws_ref[g], 0.0).astype(BF16)
            cs = slice(g * LANES, (g + 1) * LANES)
            for c in range(tm // A_CHUNK):
                rs = slice(c * A_CHUNK, (c + 1) * A_CHUNK)
                z = _dot(w, vb[rs, cs]) + bs_ref[:, cs]
                ya_ref[rs, cs] = (gu[rs, cs] * z).astype(BF16)
    ma_ref[...] = jax.nn.sigmoid(_dot(xb, wga_ref[...])) * _dot(ya_ref[...], wbra_ref[...])
    q_ref[...] = _dot(xb, wq_ref[...]) * (dk ** -0.5)
    k_ref[...] = _dot(xb, wk_ref[...])
    v_ref[...] = _dot(xb, wv_ref[...])
    sr_ref[...] = jax.nn.silu(_dot(xb, wr_ref[...]))
    sgb_ref[...] = jax.nn.sigmoid(_dot(xb, wgb_ref[...]))
    gk_low = _dot(xb, wgl_ref[...]).astype(BF16)
    log_a = _log_sigmoid(_dot(gk_low, wgk_ref[...]) + bgk_ref[...]) / GLA_TAU
    if not decode:
        pos = lax.broadcasted_iota(jnp.int32, log_a.shape, 0) % GLA_CHUNK
        shift = 1
        while shift < GLA_CHUNK:
            log_a = log_a + jnp.where(pos >= shift, pltpu.roll(log_a, shift, axis=0), 0.0)
            shift *= 2
    b_ref[...] = log_a


def _stage1(x, w, *, decode, tm, seq_len):
    t, d = x.shape
    dqk = w['wq'].shape[1]
    dv = w['wv'].shape[1]
    steps = t // tm
    row = lambda n: pl.BlockSpec((tm, n), lambda i: (i, 0))
    if decode:
        cv_shape = jax.ShapeDtypeStruct((t, d), F32)
        cv_spec = row(d)
        ws, bs = w['ws00'], w['bs0']
    else:
        per_seq = seq_len // tm
        cv_shape = jax.ShapeDtypeStruct((t // seq_len, A_CHUNK, d), F32)
        cv_spec = pl.BlockSpec((1, A_CHUNK, d), lambda i: (i // per_seq, 0, 0))
        ws, bs = w['ws'], w['bsx']
    weights = [w['wua'], w['wva'], w['wq'], w['wk'], w['wv'], w['wr'], w['wgl'], w['wgk'], w['bgk'],
               w['wga'], w['wgb'], ws, bs, w['lng'], w['lnb'], w['wbra']]
    out_shape = [jax.ShapeDtypeStruct((t, d), F32),
                 jax.ShapeDtypeStruct((t, dqk), F32),
                 jax.ShapeDtypeStruct((t, dqk), F32),
                 jax.ShapeDtypeStruct((t, dv), F32),
                 jax.ShapeDtypeStruct((t, dqk), F32),
                 jax.ShapeDtypeStruct((t, dv), F32),
                 jax.ShapeDtypeStruct((t, d), F32),
                 cv_shape]
    out_specs = [row(d), row(dqk), row(dqk), row(dv), row(dqk), row(dv), row(d), cv_spec]
    return pl.pallas_call(
        functools.partial(_stage1_kernel, decode=decode, tm=tm, dk=dqk // GLA_HEADS),
        grid=(steps,),
        in_specs=[row(d)] + [_const_spec(a.shape) for a in weights],
        out_specs=out_specs,
        out_shape=out_shape,
        scratch_shapes=[pltpu.VMEM((tm, d), BF16)],
        compiler_params=_params(("arbitrary",)),
        name="stage1_decode" if decode else "stage1_prompt",
    )(x, *weights)


def _block_ref_rows(b, w):
    c, n = b.shape
    if w >= 8:
        r = b.reshape(c // (2 * w), 2 * w, n)
        return jnp.broadcast_to(r[:, w - 1:w, :], r.shape).reshape(c, n)
    r = b.reshape(c // 8, 8, n)
    sub = lax.broadcasted_iota(jnp.int32, r.shape, 1)
    pick = lambda j: jnp.broadcast_to(r[:, j:j + 1, :], r.shape)
    if w == 4:
        out = pick(3)
    elif w == 2:
        out = jnp.where(sub < 4, pick(1), pick(5))
    else:
        out = jnp.where(sub < 2, pick(0), jnp.where(sub < 4, pick(2), jnp.where(sub < 6, pick(4), pick(6))))
    return out.reshape(c, n)


def _gla_kernel(q_ref, k_ref, b_ref, v_ref, sr_ref, gam_ref, yb_ref, sfin_ref, s_ref, *, dk, dv):
    c = pl.program_id(1)

    @pl.when(c == 0)
    def _():
        s_ref[...] = jnp.zeros_like(s_ref)

    cc = GLA_CHUNK
    row = lax.broadcasted_iota(jnp.int32, (cc, cc), 0)
    col = lax.broadcasted_iota(jnp.int32, (cc, cc), 1)
    rowk = lax.broadcasted_iota(jnp.int32, (cc, dk), 0)
    for h in range(GLA_HEADS):
        ks = slice(h * dk, (h + 1) * dk)
        vs = slice(h * dv, (h + 1) * dv)
        qh, kh, bh = q_ref[:, ks], k_ref[:, ks], b_ref[:, ks]
        vb = v_ref[:, vs].astype(BF16)
        s_old = s_ref[h]
        o = _dot((qh * jnp.exp(bh)).astype(BF16), s_old.astype(BF16))
        att = jnp.where(row == col, jnp.sum(qh * kh, axis=1, keepdims=True), 0.0)
        w = cc // 2
        while w >= 1:
            bref = _block_ref_rows(bh, w)
            second = (rowk // w) % 2 == 1
            ql = jnp.where(second, qh * jnp.exp(jnp.minimum(bh - bref, 0.0)), 0.0)
            kl = jnp.where(second, 0.0, kh * jnp.exp(jnp.minimum(bref - bh, 0.0)))
            a_l = _dot_nt(ql.astype(BF16), kl.astype(BF16))
            att = att + jnp.where(row // (2 * w) == col // (2 * w), a_l, 0.0)
            w //= 2
        o = o + _dot(att.astype(BF16), vb)
        b_last = bh[cc - 1:cc, :]
        kdec_t = (kh * jnp.exp(b_last - bh)).T.astype(BF16)
        decay_col = jnp.exp(bh.T[:, cc - 1:cc])
        s_new = decay_col * s_old + _dot(kdec_t, vb)
        s_ref[h] = s_new
        o = o * lax.rsqrt(jnp.mean(o * o, axis=-1, keepdims=True) + LN_EPS) * gam_ref[...]
        yb_ref[:, vs] = (sr_ref[:, vs] * o).astype(BF16)

    @pl.when(c == pl.num_programs(1) - 1)
    def _():
        sfin_ref[0] = s_ref[...]


def _gla_prompt(q, k, b, v, sr, gamma, *, bsz, seq_len):
    t, dqk = q.shape
    dvt = v.shape[1]
    dk, dv = dqk // GLA_HEADS, dvt // GLA_HEADS
    nc = seq_len // GLA_CHUNK
    row = lambda n: pl.BlockSpec((GLA_CHUNK, n), lambda i, c: (i * nc + c, 0))
    return pl.pallas_call(
        functools.partial(_gla_kernel, dk=dk, dv=dv),
        grid=(bsz, nc),
        in_specs=[row(dqk), row(dqk), row(dqk), row(dvt), row(dvt), _const_spec(gamma.shape)],
        out_specs=[row(dvt), pl.BlockSpec((1, GLA_HEADS, dk, dv), lambda i, c: (i, 0, 0, 0))],
        out_shape=[jax.ShapeDtypeStruct((t, dvt), BF16),
                   jax.ShapeDtypeStruct((bsz, GLA_HEADS, dk, dv), F32)],
        scratch_shapes=[pltpu.VMEM((GLA_HEADS, dk, dv), F32)],
        compiler_params=_params(("arbitrary", "arbitrary")),
        name="gla_prompt",
    )(q, k, b, v, sr, gamma)


DEC_TOKENS = 8


def _gla_step_kernel(qt_ref, kt_ref, gt_ref, v_ref, sr_ref, gam_ref, s_ref, yb_ref, sout_ref, *, dk, dv):
    for j in range(DEC_TOKENS):
        for h in range(GLA_HEADS):
            ks = slice(h * dk, (h + 1) * dk)
            vs = slice(h * dv, (h + 1) * dv)
            decay = jnp.exp(gt_ref[0, ks, j:j + 1])
            s_new = decay * s_ref[j, h] + kt_ref[0, ks, j:j + 1] * v_ref[j:j + 1, vs]
            sout_ref[j, h] = s_new
            o = jnp.sum(qt_ref[0, ks, j:j + 1] * s_new, axis=0, keepdims=True)
            o = o * lax.rsqrt(jnp.mean(o * o, axis=-1, keepdims=True) + LN_EPS) * gam_ref[...]
            yb_ref[j:j + 1, vs] = (sr_ref[j:j + 1, vs] * o).astype(BF16)


def _gla_step(q, k, g, v, sr, gamma, state):
    t, dqk = q.shape
    dvt = v.shape[1]
    dk, dv = dqk // GLA_HEADS, dvt // GLA_HEADS
    steps = t // DEC_TOKENS
    tr = lambda a: a.reshape(steps, DEC_TOKENS, dqk).transpose(0, 2, 1)
    col = pl.BlockSpec((1, dqk, DEC_TOKENS), lambda i: (i, 0, 0))
    row = lambda n: pl.BlockSpec((DEC_TOKENS, n), lambda i: (i, 0))
    st = pl.BlockSpec((DEC_TOKENS, GLA_HEADS, dk, dv), lambda i: (i, 0, 0, 0))
    return pl.pallas_call(
        functools.partial(_gla_step_kernel, dk=dk, dv=dv),
        grid=(steps,),
        in_specs=[col, col, col, row(dvt), row(dvt), _const_spec(gamma.shape), st],
        out_specs=[row(dvt), st],
        out_shape=[jax.ShapeDtypeStruct((t, dvt), BF16), jax.ShapeDtypeStruct(state.shape, F32)],
        compiler_params=_params(("arbitrary",)),
        name="gla_step",
    )(tr(q), tr(k), tr(g), v, sr, gamma, state)


def _merge_kernel(ma_ref, sgb_ref, yb_ref, x_ref, wbrb_ref, wo_ref, g_ref, b_ref, x1_ref, x1t_ref, *, alpha):
    m = ma_ref[...] + sgb_ref[...] * _dot(yb_ref[...], wbrb_ref[...])
    x1 = _layernorm(alpha * x_ref[...] + _dot(m.astype(BF16), wo_ref[...]), g_ref[...], b_ref[...])
    x1_ref[...] = x1
    x1t_ref[...] = x1.T.astype(BF16)


def _merge(ma, sgb, yb, x, w, *, tm, alpha):
    t, d = x.shape
    row = pl.BlockSpec((tm, d), lambda i: (i, 0))
    weights = [w['wbrb'], w['wo'], w['ln1g'], w['ln1b']]
    return pl.pallas_call(
        functools.partial(_merge_kernel, alpha=alpha),
        grid=(t // tm,),
        in_specs=[row, row, row, row] + [_const_spec(a.shape) for a in weights],
        out_specs=[row, pl.BlockSpec((d, tm), lambda i: (0, i))],
        out_shape=[jax.ShapeDtypeStruct((t, d), F32), jax.ShapeDtypeStruct((d, t), BF16)],
        compiler_params=_params(("arbitrary",)),
        name="merge_ln1",
    )(ma, sgb, yb, x, *weights)


CAND_COUNTS = tuple(PEER_TOPK // (r1 + 1) for r1 in range(PEER_TOPK))
CAND_OFFSETS = tuple(sum(CAND_COUNTS[:r1]) for r1 in range(PEER_TOPK))
CAND_ROWS = -(-sum(CAND_COUNTS) // 8) * 8


def _select_top(s, iota, vals_ref, exact_ties):
    n = s.shape[0]
    rank = jnp.full(s.shape, NOT_SELECTED, F32)
    for r in range(PEER_TOPK):
        m = jnp.max(s, axis=0, keepdims=True)
        sel = s == m
        if exact_ties:
            first = jnp.min(jnp.where(sel, iota, float(n)), axis=0, keepdims=True)
            sel = iota == first
        rank = jnp.where(sel, float(r), rank)
        vals_ref[r:r + 1, :] = m
        s = jnp.where(sel, -jnp.inf, s)
    return rank


def _count_selected(rank):
    return jnp.sum(jnp.where(rank < NOT_SELECTED, 1.0, 0.0), axis=0, keepdims=True)


def _route_head(s_ref, rank_ref, vals_ref, cand_ref, z_ref, exact_ties):
    tm = s_ref.shape[2]
    kk = PEER_TOPK
    bad = jnp.zeros((1, tm), F32)
    for p in range(2):
        iota_n = lax.broadcasted_iota(jnp.int32, (PEER_NKEYS, tm), 0).astype(F32) if exact_ties else None
        rank = _select_top(s_ref[p], iota_n, vals_ref.at[p], exact_ties)
        rank_ref[p] = rank
        bad = jnp.maximum(bad, jnp.abs(_count_selected(rank) - kk))
    cand_ref[...] = jnp.full(cand_ref.shape, -jnp.inf, F32)
    for r1 in range(kk):
        n2, off = CAND_COUNTS[r1], CAND_OFFSETS[r1]
        cand_ref[off:off + n2, :] = vals_ref[0, r1:r1 + 1, :] + vals_ref[1, 0:n2, :]
    cand = cand_ref[...]
    iota_c = lax.broadcasted_iota(jnp.int32, cand.shape, 0).astype(F32) if exact_ties else None
    best = vals_ref[0, 0:1, :] + vals_ref[1, 0:1, :]
    rank_c = jnp.full(cand.shape, NOT_SELECTED, F32)
    z = jnp.zeros((1, tm), F32)
    for r in range(kk):
        m = jnp.max(cand, axis=0, keepdims=True)
        sel = cand == m
        if exact_ties:
            first = jnp.min(jnp.where(sel, iota_c, float(CAND_ROWS)), axis=0, keepdims=True)
            sel = iota_c == first
        rank_c = jnp.where(sel, float(r), rank_c)
        z = z + jnp.exp(m - best)
        cand = jnp.where(sel, -jnp.inf, cand)
    cand_ref[...] = jnp.where(rank_c < NOT_SELECTED, 1.0, 0.0)
    z_ref[...] = z
    return jnp.maximum(bad, jnp.abs(_count_selected(rank_c) - kk))


def _route_kernel(x1_ref, wq_ref, keys_ref, rank2_ref, e2_ref, cnt_ref, c_ref,
                  qn_ref, s_ref, rank_ref, vals_ref, cand_ref, z_ref):
    tm = x1_ref.shape[0]
    kk = PEER_TOPK
    q = _dot(x1_ref[...].astype(BF16), wq_ref[...])
    for hp in range(2 * PEER_HEADS):
        qq = q[:, hp * LANES:(hp + 1) * LANES]
        qc = qq - jnp.mean(qq, axis=-1, keepdims=True)
        qn_ref[hp] = (qc * lax.rsqrt(jnp.mean(qc * qc, axis=-1, keepdims=True) + LN_EPS)).astype(BF16)

    def head(h, carry):
        for p in range(2):
            s_ref[p] = _dot_nt(keys_ref[2 * h + p], qn_ref[2 * h + p])
        bad = _route_head(s_ref, rank_ref, vals_ref, cand_ref, z_ref, exact_ties=False)

        @pl.when(jnp.max(bad) > 0.0)
        def _():
            _route_head(s_ref, rank_ref, vals_ref, cand_ref, z_ref, exact_ties=True)

        rank1 = rank_ref[0]
        cnt = jnp.zeros((PEER_NKEYS, tm), F32)
        for r1 in range(kk):
            n2, off = CAND_COUNTS[r1], CAND_OFFSETS[r1]
            count = jnp.sum(cand_ref[off:off + n2, :], axis=0, keepdims=True)
            cnt = cnt + jnp.where(rank1 == float(r1), count, 0.0)
        rank2_ref[h] = rank_ref[1].astype(BF16)
        cnt_ref[h] = cnt
        c_ref[h] = jnp.exp(s_ref[0] - vals_ref[0, 0:1, :]) / z_ref[...]
        e2_ref[h] = jnp.exp(s_ref[1] - vals_ref[1, 0:1, :]).astype(BF16)
        return carry

    lax.fori_loop(0, PEER_HEADS, head, 0)


def _route(x1, wq, keys, *, tm):
    t, d = x1.shape
    fac = pl.BlockSpec((PEER_HEADS, PEER_NKEYS, tm), lambda i: (0, 0, i))
    shape = lambda dt: jax.ShapeDtypeStruct((PEER_HEADS, PEER_NKEYS, t), dt)
    return pl.pallas_call(
        _route_kernel,
        grid=(t // tm,),
        in_specs=[pl.BlockSpec((tm, d), lambda i: (i, 0)), _const_spec(wq.shape), _const_spec(keys.shape)],
        out_specs=[fac, fac, fac, fac],
        out_shape=[shape(BF16), shape(BF16), shape(F32), shape(F32)],
        scratch_shapes=[pltpu.VMEM((2 * PEER_HEADS, tm, LANES), BF16),
                        pltpu.VMEM((2, PEER_NKEYS, tm), F32),
                        pltpu.VMEM((2, PEER_NKEYS, tm), F32),
                        pltpu.VMEM((2, PEER_TOPK, tm), F32),
                        pltpu.VMEM((CAND_ROWS, tm), F32),
                        pltpu.VMEM((1, tm), F32)],
        compiler_params=_params(("arbitrary",)),
        name="peer_route",
    )(x1, wq, keys)


GROUP = 2


def _experts_kernel(x1t_ref, u_ref, vt_ref, rank2_ref, e2_ref, cnt_ref, c_ref, o_ref, acc_ref, wt_ref, *, na):
    j = pl.program_id(1)

    @pl.when(j == 0)
    def _():
        acc_ref[...] = jnp.zeros_like(acc_ref)

    for g in range(0, na, GROUP):
        rows = slice(g * PEER_NKEYS, (g + GROUP) * PEER_NKEYS)
        ht = jax.nn.gelu(_dot(u_ref[rows, :], x1t_ref[...]))
        for a in range(g, g + GROUP):
            gate = None
            for h in range(PEER_HEADS):
                cnt_row = cnt_ref[h, pl.ds(j * na + a, 1), :].astype(BF16)
                c_row = c_ref[h, pl.ds(j * na + a, 1), :].astype(BF16)
                term = jnp.where(rank2_ref[h] < cnt_row, e2_ref[h] * c_row, jnp.zeros((), BF16))
                gate = term if gate is None else gate + term
            sub = slice((a - g) * PEER_NKEYS, (a - g + 1) * PEER_NKEYS)
            wt_ref[a * PEER_NKEYS:(a + 1) * PEER_NKEYS, :] = ht[sub, :].astype(BF16) * gate
    acc_ref[...] += _dot(vt_ref[...], wt_ref[...])

    @pl.when(j == pl.num_programs(1) - 1)
    def _():
        o_ref[...] = acc_ref[...].T


def _experts(x1t, u, vt, rank2, e2, cnt, c, *, tm, tn):
    d, t = x1t.shape
    n = u.shape[0]
    na = tn // PEER_NKEYS
    fac = pl.BlockSpec((PEER_HEADS, PEER_NKEYS, tm), lambda i, j: (0, 0, i))
    return pl.pallas_call(
        functools.partial(_experts_kernel, na=na),
        grid=(t // tm, n // tn),
        in_specs=[pl.BlockSpec((d, tm), lambda i, j: (0, i)),
                  pl.BlockSpec((tn, d), lambda i, j: (j, 0)),
                  pl.BlockSpec((d, tn), lambda i, j: (0, j)),
                  fac, fac, fac, fac],
        out_specs=pl.BlockSpec((tm, d), lambda i, j: (i, 0)),
        out_shape=jax.ShapeDtypeStruct((t, d), F32),
        scratch_shapes=[pltpu.VMEM((d, tm), F32), pltpu.VMEM((tn, tm), BF16)],
        compiler_params=_params(("arbitrary", "arbitrary")),
        name="peer_experts",
    )(x1t, u, vt, rank2, e2, cnt, c)


def _final_kernel(x1_ref, peer_ref, p_ref, g_ref, b_ref, wpe_ref, wpg_ref, o_ref, *, alpha):
    x2 = _layernorm(alpha * x1_ref[...] + peer_ref[...], g_ref[...], b_ref[...])
    emb = _dot(p_ref[...].astype(BF16), wpe_ref[...])
    o_ref[...] = x2 + emb * jax.nn.sigmoid(_dot(x2.astype(BF16), wpg_ref[...]))


def _final(x1, peer, p, w, *, tm, alpha):
    t, d = x1.shape
    row = pl.BlockSpec((tm, d), lambda i: (i, 0))
    weights = [w['ln2g'], w['ln2b'], w['wpe'], w['wpg']]
    return pl.pallas_call(
        functools.partial(_final_kernel, alpha=alpha),
        grid=(t // tm,),
        in_specs=[row, row, pl.BlockSpec((tm, p.shape[1]), lambda i: (i, 0))]
                 + [_const_spec(a.shape) for a in weights],
        out_specs=row,
        out_shape=jax.ShapeDtypeStruct((t, d), F32),
        compiler_params=_params(("arbitrary",)),
        name="ln2_embed",
    )(x1, peer, p, *weights)


def _prep_weights(w_in, w_s, b_s, ln_v_g, ln_v_b, w_gk, b_gk, gla_norm_g, w_br_a, w_br_b, w_o, ln1_g, ln1_b,
                  peer_wq, peer_keys, peer_u, peer_v, ln2_g, ln2_b, w_pe, w_pg):
    d = w_in.shape[0]
    aw = ln_v_g.shape[0]
    dqk = w_gk.shape[1]
    dvt = w_br_b.shape[0]
    rank = w_gk.shape[0]
    sizes = (aw, aw, dqk, dqk, dvt, dvt, rank, d, d)
    offs = [0]
    for s in sizes:
        offs.append(offs[-1] + s)
    cols = lambda i: w_in[:, offs[i]:offs[i + 1]].astype(BF16)
    r2 = lambda a: a.reshape(1, -1)
    gdim = aw // A_GROUPS
    nk = peer_keys.shape[2]
    return {
        'wua': cols(0), 'wva': cols(1), 'wq': cols(2), 'wk': cols(3), 'wv': cols(4), 'wr': cols(5),
        'wgl': jnp.pad(cols(6), ((0, 0), (0, LANES - rank))),
        'wgk': jnp.pad(w_gk.astype(BF16), ((0, LANES - rank), (0, 0))),
        'bgk': r2(b_gk), 'wga': cols(7), 'wgb': cols(8),
        'ws': w_s,
        'bsx': jnp.repeat(b_s.T, gdim, axis=1),
        'ws00': r2(jnp.repeat(w_s[:, 0, 0], gdim)),
        'bs0': r2(jnp.repeat(b_s[:, 0], gdim)),
        'lng': r2(ln_v_g), 'lnb': r2(ln_v_b),
        'wbra': w_br_a.astype(BF16), 'wbrb': w_br_b.astype(BF16), 'wo': w_o.astype(BF16),
        'gamma': r2(gla_norm_g), 'ln1g': r2(ln1_g), 'ln1b': r2(ln1_b),
        'pwq': peer_wq.astype(BF16),
        'keys': peer_keys.astype(BF16).reshape(-1, nk, peer_keys.shape[3]),
        'pu': peer_u.astype(BF16), 'pvt': peer_v.astype(BF16).T,
        'ln2g': r2(ln2_g), 'ln2b': r2(ln2_b), 'wpe': w_pe.astype(BF16), 'wpg': w_pg.astype(BF16),
    }


def _block(t, cap):
    best = LANES
    for m in range(LANES, cap + 1, LANES):
        if t % m == 0:
            best = m
    return best


def _layer(x, p, state, w, *, alpha):
    bsz, seq_len, d = x.shape
    t = bsz * seq_len
    x2d = x.reshape(t, d)
    decode = seq_len == 1
    assert decode or (seq_len % A_CHUNK == 0 and state is None)
    assert t % LANES == 0
    if decode:
        ma, q, k, v, b, sr, sgb, cv = _stage1(x2d, w, decode=True, tm=_block(t, 256), seq_len=1)
        yb, s_new = _gla_step(q, k, b, v, sr, w['gamma'], state)
        cv = cv.reshape(bsz, 1, d)
    else:
        ma, q, k, v, b, sr, sgb, cv = _stage1(x2d, w, decode=False, tm=_block(seq_len, 256), seq_len=seq_len)
        yb, s_new = _gla_prompt(q, k, b, v, sr, w['gamma'], bsz=bsz, seq_len=seq_len)
    x1, x1t = _merge(ma, sgb, yb, x2d, w, tm=_block(t, 512), alpha=alpha)
    rank2, e2, cnt, c = _route(x1, w['pwq'], w['keys'], tm=LANES)
    peer = _experts(x1t, w['pu'], w['pvt'], rank2, e2, cnt, c, tm=_block(t, 512), tn=512)
    x3 = _final(x1, peer, p.reshape(t, -1), w, tm=_block(t, 512), alpha=alpha)
    return x3.reshape(bsz, seq_len, d), s_new, cv


def kernel(x_prompt, x_sample, state_gla, p_prompt, p_sample, w_in, w_s, b_s, ln_v_g, ln_v_b, w_gk, b_gk,
           gla_norm_g, w_br_a, w_br_b, w_o, ln1_g, ln1_b, peer_wq, peer_keys, peer_u, peer_v, ln2_g, ln2_b,
           w_pe, w_pg):
    depth = w_in.shape[0]
    alpha = (2.0 * depth) ** 0.25
    yp, ys = x_prompt, x_sample
    gla_p, gla_s, cv_p, cv_s = [], [], [], []
    for i in range(depth):
        w = _prep_weights(w_in[i], w_s[i], b_s[i], ln_v_g[i], ln_v_b[i], w_gk[i], b_gk[i], gla_norm_g[i],
                          w_br_a[i], w_br_b[i], w_o[i], ln1_g[i], ln1_b[i], peer_wq[i], peer_keys[i],
                          peer_u[i], peer_v[i], ln2_g[i], ln2_b[i], w_pe[i], w_pg[i])
        yp, sp, vp = _layer(yp, p_prompt[i], None, w, alpha=alpha)
        ys, ss, vs = _layer(ys, p_sample[i], state_gla[i], w, alpha=alpha)
        gla_p.append(sp)
        gla_s.append(ss)
        cv_p.append(vp)
        cv_s.append(vs)
    return (yp, ys, jnp.stack(gla_p), jnp.stack(gla_s), jnp.stack(cv_p), jnp.stack(cv_s))
```

```python
import functools
import math

import jax
import jax.numpy as jnp
from jax import lax
from jax.experimental import pallas as pl
from jax.experimental.pallas import tpu as pltpu

F32 = jnp.float32
BF16 = jnp.bfloat16

LANES = 128
A_CHUNK = 128
A_GROUPS = 8
GLA_HEADS = 4
GLA_TAU = 16.0
GLA_CHUNK = 128
PEER_HEADS = 8
PEER_NKEYS = 128
PEER_TOPK = 16
LN_EPS = 1e-5
NOT_SELECTED = 255.0
VMEM_LIMIT = 56 * 1024 * 1024


def _dot(a, b):
    return jnp.dot(a, b, preferred_element_type=F32)


def _dot_nt(a, b):
    return lax.dot_general(a, b, (((1,), (1,)), ((), ())), preferred_element_type=F32)


def _layernorm(x, g, b):
    mu = jnp.mean(x, axis=-1, keepdims=True)
    xc = x - mu
    var = jnp.mean(xc * xc, axis=-1, keepdims=True)
    return xc * lax.rsqrt(var + LN_EPS) * g + b


def _log_sigmoid(z):
    return jnp.minimum(z, 0.0) - jnp.log1p(jnp.exp(-jnp.abs(z)))


def _const_spec(shape):
    nd = len(shape)
    return pl.BlockSpec(shape, lambda *_: (0,) * nd, pipeline_mode=pl.Buffered(1))


def _params(semantics):
    return pltpu.CompilerParams(dimension_semantics=semantics, vmem_limit_bytes=VMEM_LIMIT)


def _stage1_kernel(x_ref, wua_ref, wva_ref, wq_ref, wk_ref, wv_ref, wr_ref, wgl_ref, wgk_ref, bgk_ref,
                   wga_ref, wgb_ref, ws_ref, bs_ref, lng_ref, lnb_ref, wbra_ref,
                   ma_ref, q_ref, k_ref, v_ref, b_ref, sr_ref, sgb_ref, cv_ref, ya_ref, *, decode, tm, dk):
    xb = x_ref[...].astype(BF16)
    gu = jax.nn.gelu(_dot(xb, wua_ref[...]))
    van = _layernorm(jax.nn.gelu(_dot(xb, wva_ref[...])), lng_ref[...], lnb_ref[...])
    if decode:
        cv_ref[...] = van
        ya_ref[...] = (gu * (van * ws_ref[...] + bs_ref[...])).astype(BF16)
    else:
        cv_ref[0] = van[tm - A_CHUNK:, :]
        vb = van.astype(BF16)
        row = lax.broadcasted_iota(jnp.int32, (A_CHUNK, A_CHUNK), 0)
        col = lax.broadcasted_iota(jnp.int32, (A_CHUNK, A_CHUNK), 1)
        for g in range(A_GROUPS):
            w = jnp.where(row >= col, ws_ref[g], 0.0).astype(BF16)
            cs = slice(g * LANES, (g + 1) * LANES)
            for c in range(tm // A_CHUNK):
                rs = slice(c * A_CHUNK, (c + 1) * A_CHUNK)
                z = _dot(w, vb[rs, cs]) + bs_ref[:, cs]
                ya_ref[rs, cs] = (gu[rs, cs] * z).astype(BF16)
    ma_ref[...] = jax.nn.sigmoid(_dot(xb, wga_ref[...])) * _dot(ya_ref[...], wbra_ref[...])
    q_ref[...] = _dot(xb, wq_ref[...]) * (dk ** -0.5)
    k_ref[...] = _dot(xb, wk_ref[...])
    v_ref[...] = _dot(xb, wv_ref[...])
    sr_ref[...] = jax.nn.silu(_dot(xb, wr_ref[...]))
    sgb_ref[...] = jax.nn.sigmoid(_dot(xb, wgb_ref[...]))
    gk_low = _dot(xb, wgl_ref[...]).astype(BF16)
    log_a = _log_sigmoid(_dot(gk_low, wgk_ref[...]) + bgk_ref[...]) / GLA_TAU
    if not decode:
        pos = lax.broadcasted_iota(jnp.int32, log_a.shape, 0) % GLA_CHUNK
        shift = 1
        while shift < GLA_CHUNK:
            log_a = log_a + jnp.where(pos >= shift, pltpu.roll(log_a, shift, axis=0), 0.0)
            shift *= 2
    b_ref[...] = log_a


def _stage1(x, w, *, decode, tm, seq_len):
    t, d = x.shape
    dqk = w['wq'].shape[1]
    dv = w['wv'].shape[1]
    steps = t // tm
    row = lambda n: pl.BlockSpec((tm, n), lambda i: (i, 0))
    if decode:
        cv_shape = jax.ShapeDtypeStruct((t, d), F32)
        cv_spec = row(d)
        ws, bs = w['ws00'], w['bs0']
    else:
        per_seq = seq_len // tm
        cv_shape = jax.ShapeDtypeStruct((t // seq_len, A_CHUNK, d), F32)
        cv_spec = pl.BlockSpec((1, A_CHUNK, d), lambda i: (i // per_seq, 0, 0))
        ws, bs = w['ws'], w['bsx']
    weights = [w['wua'], w['wva'], w['wq'], w['wk'], w['wv'], w['wr'], w['wgl'], w['wgk'], w['bgk'],
               w['wga'], w['wgb'], ws, bs, w['lng'], w['lnb'], w['wbra']]
    out_shape = [jax.ShapeDtypeStruct((t, d), F32),
                 jax.ShapeDtypeStruct((t, dqk), F32),
                 jax.ShapeDtypeStruct((t, dqk), F32),
                 jax.ShapeDtypeStruct((t, dv), F32),
                 jax.ShapeDtypeStruct((t, dqk), F32),
                 jax.ShapeDtypeStruct((t, dv), F32),
                 jax.ShapeDtypeStruct((t, d), F32),
                 cv_shape]
    out_specs = [row(d), row(dqk), row(dqk), row(dv), row(dqk), row(dv), row(d), cv_spec]
    return pl.pallas_call(
        functools.partial(_stage1_kernel, decode=decode, tm=tm, dk=dqk // GLA_HEADS),
        grid=(steps,),
        in_specs=[row(d)] + [_const_spec(a.shape) for a in weights],
        out_specs=out_specs,
        out_shape=out_shape,
        scratch_shapes=[pltpu.VMEM((tm, d), BF16)],
        compiler_params=_params(("arbitrary",)),
        name="stage1_decode" if decode else "stage1_prompt",
    )(x, *weights)


def _block_ref_rows(b, w):
    c, n = b.shape
    if w >= 8:
        r = b.reshape(c // (2 * w), 2 * w, n)
        return jnp.broadcast_to(r[:, w - 1:w, :], r.shape).reshape(c, n)
    r = b.reshape(c // 8, 8, n)
    sub = lax.broadcasted_iota(jnp.int32, r.shape, 1)
    pick = lambda j: jnp.broadcast_to(r[:, j:j + 1, :], r.shape)
    if w == 4:
        out = pick(3)
    elif w == 2:
        out = jnp.where(sub < 4, pick(1), pick(5))
    else:
        out = jnp.where(sub < 2, pick(0), jnp.where(sub < 4, pick(2), jnp.where(sub < 6, pick(4), pick(6))))
    return out.reshape(c, n)


def _gla_kernel(q_ref, k_ref, b_ref, v_ref, sr_ref, gam_ref, yb_ref, sfin_ref, s_ref, *, dk, dv):
    c = pl.program_id(1)

    @pl.when(c == 0)
    def _():
        s_ref[...] = jnp.zeros_like(s_ref)

    cc = GLA_CHUNK
    row = lax.broadcasted_iota(jnp.int32, (cc, cc), 0)
    col = lax.broadcasted_iota(jnp.int32, (cc, cc), 1)
    rowk = lax.broadcasted_iota(jnp.int32, (cc, dk), 0)
    for h in range(GLA_HEADS):
        ks = slice(h * dk, (h + 1) * dk)
        vs = slice(h * dv, (h + 1) * dv)
        qh, kh, bh = q_ref[:, ks], k_ref[:, ks], b_ref[:, ks]
        vb = v_ref[:, vs].astype(BF16)
        s_old = s_ref[h]
        o = _dot((qh * jnp.exp(bh)).astype(BF16), s_old.astype(BF16))
        att = jnp.where(row == col, jnp.sum(qh * kh, axis=1, keepdims=True), 0.0)
        w = cc // 2
        while w >= 1:
            bref = _block_ref_rows(bh, w)
            second = (rowk // w) % 2 == 1
            ql = jnp.where(second, qh * jnp.exp(jnp.minimum(bh - bref, 0.0)), 0.0)
            kl = jnp.where(second, 0.0, kh * jnp.exp(jnp.minimum(bref - bh, 0.0)))
            a_l = _dot_nt(ql.astype(BF16), kl.astype(BF16))
            att = att + jnp.where(row // (2 * w) == col // (2 * w), a_l, 0.0)
            w //= 2
        o = o + _dot(att.astype(BF16), vb)
        b_last = bh[cc - 1:cc, :]
        kdec_t = (kh * jnp.exp(b_last - bh)).T.astype(BF16)
        decay_col = jnp.exp(bh.T[:, cc - 1:cc])
        s_new = decay_col * s_old + _dot(kdec_t, vb)
        s_ref[h] = s_new
        o = o * lax.rsqrt(jnp.mean(o * o, axis=-1, keepdims=True) + LN_EPS) * gam_ref[...]
        yb_ref[:, vs] = (sr_ref[:, vs] * o).astype(BF16)

    @pl.when(c == pl.num_programs(1) - 1)
    def _():
        sfin_ref[0] = s_ref[...]


def _gla_prompt(q, k, b, v, sr, gamma, *, bsz, seq_len):
    t, dqk = q.shape
    dvt = v.shape[1]
    dk, dv = dqk // GLA_HEADS, dvt // GLA_HEADS
    nc = seq_len // GLA_CHUNK
    row = lambda n: pl.BlockSpec((GLA_CHUNK, n), lambda i, c: (i * nc + c, 0))
    return pl.pallas_call(
        functools.partial(_gla_kernel, dk=dk, dv=dv),
        grid=(bsz, nc),
        in_specs=[row(dqk), row(dqk), row(dqk), row(dvt), row(dvt), _const_spec(gamma.shape)],
        out_specs=[row(dvt), pl.BlockSpec((1, GLA_HEADS, dk, dv), lambda i, c: (i, 0, 0, 0))],
        out_shape=[jax.ShapeDtypeStruct((t, dvt), BF16),
                   jax.ShapeDtypeStruct((bsz, GLA_HEADS, dk, dv), F32)],
        scratch_shapes=[pltpu.VMEM((GLA_HEADS, dk, dv), F32)],
        compiler_params=_params(("arbitrary", "arbitrary")),
        name="gla_prompt",
    )(q, k, b, v, sr, gamma)


DEC_TOKENS = 8


def _gla_step_kernel(qt_ref, kt_ref, gt_ref, v_ref, sr_ref, gam_ref, s_ref, yb_ref, sout_ref, *, dk, dv):
    for j in range(DEC_TOKENS):
        for h in range(GLA_HEADS):
            ks = slice(h * dk, (h + 1) * dk)
            vs = slice(h * dv, (h + 1) * dv)
            decay = jnp.exp(gt_ref[0, ks, j:j + 1])
            s_new = decay * s_ref[j, h] + kt_ref[0, ks, j:j + 1] * v_ref[j:j + 1, vs]
            sout_ref[j, h] = s_new
            o = jnp.sum(qt_ref[0, ks, j:j + 1] * s_new, axis=0, keepdims=True)
            o = o * lax.rsqrt(jnp.mean(o * o, axis=-1, keepdims=True) + LN_EPS) * gam_ref[...]
            yb_ref[j:j + 1, vs] = (sr_ref[j:j + 1, vs] * o).astype(BF16)


def _gla_step(q, k, g, v, sr, gamma, state):
    t, dqk = q.shape
    dvt = v.shape[1]
    dk, dv = dqk // GLA_HEADS, dvt // GLA_HEADS
    steps = t // DEC_TOKENS
    tr = lambda a: a.reshape(steps, DEC_TOKENS, dqk).transpose(0, 2, 1)
    col = pl.BlockSpec((1, dqk, DEC_TOKENS), lambda i: (i, 0, 0))
    row = lambda n: pl.BlockSpec((DEC_TOKENS, n), lambda i: (i, 0))
    st = pl.BlockSpec((DEC_TOKENS, GLA_HEADS, dk, dv), lambda i: (i, 0, 0, 0))
    return pl.pallas_call(
        functools.partial(_gla_step_kernel, dk=dk, dv=dv),
        grid=(steps,),
        in_specs=[col, col, col, row(dvt), row(dvt), _const_spec(gamma.shape), st],
        out_specs=[row(dvt), st],
        out_shape=[jax.ShapeDtypeStruct((t, dvt), BF16), jax.ShapeDtypeStruct(state.shape, F32)],
        compiler_params=_params(("arbitrary",)),
        name="gla_step",
    )(tr(q), tr(k), tr(g), v, sr, gamma, state)


def _merge_kernel(ma_ref, sgb_ref, yb_ref, x_ref, wbrb_ref, wo_ref, g_ref, b_ref, x1_ref, x1t_ref, *, alpha):
    m = ma_ref[...] + sgb_ref[...] * _dot(yb_ref[...], wbrb_ref[...])
    x1 = _layernorm(alpha * x_ref[...] + _dot(m.astype(BF16), wo_ref[...]), g_ref[...], b_ref[...])
    x1_ref[...] = x1
    x1t_ref[...] = x1.T.astype(BF16)


def _merge(ma, sgb, yb, x, w, *, tm, alpha):
    t, d = x.shape
    row = pl.BlockSpec((tm, d), lambda i: (i, 0))
    weights = [w['wbrb'], w['wo'], w['ln1g'], w['ln1b']]
    return pl.pallas_call(
        functools.partial(_merge_kernel, alpha=alpha),
        grid=(t // tm,),
        in_specs=[row, row, row, row] + [_const_spec(a.shape) for a in weights],
        out_specs=[row, pl.BlockSpec((d, tm), lambda i: (0, i))],
        out_shape=[jax.ShapeDtypeStruct((t, d), F32), jax.ShapeDtypeStruct((d, t), BF16)],
        compiler_params=_params(("arbitrary",)),
        name="merge_ln1",
    )(ma, sgb, yb, x, *weights)


CAND_COUNTS = tuple(PEER_TOPK // (r1 + 1) for r1 in range(PEER_TOPK))
CAND_OFFSETS = tuple(sum(CAND_COUNTS[:r1]) for r1 in range(PEER_TOPK))
CAND_ROWS = -(-sum(CAND_COUNTS) // 8) * 8


SUBLANES = 8
KEY_TILES = PEER_NKEYS // SUBLANES
assert KEY_TILES == PEER_TOPK


def _odd_even_merge(lo, hi, r):
    step = r * 2
    if step < hi - lo:
        yield from _odd_even_merge(lo, hi, step)
        yield from _odd_even_merge(lo + r, hi, step)
        yield from [(i, i + r) for i in range(lo + r, hi - r, step)]
    else:
        yield (lo, lo + r)


def _odd_even_merge_sort(lo, hi):
    if hi - lo >= 1:
        mid = lo + (hi - lo) // 2
        yield from _odd_even_merge_sort(lo, mid)
        yield from _odd_even_merge_sort(mid + 1, hi)
        yield from _odd_even_merge(lo, hi, 1)


SORT_NET = tuple(_odd_even_merge_sort(0, KEY_TILES - 1))


def _compare_exchange(v, i, j):
    v[i], v[j] = jnp.maximum(v[i], v[j]), jnp.minimum(v[i], v[j])


def _sorted_top(s3):
    v = [s3[i] for i in range(KEY_TILES)]
    for i, j in SORT_NET:
        _compare_exchange(v, i, j)
    shift = SUBLANES // 2
    while shift >= 1:
        other = [pltpu.roll(x, shift, axis=0) for x in v]
        v = [jnp.maximum(v[i], other[PEER_TOPK - 1 - i]) for i in range(PEER_TOPK)]
        d = PEER_TOPK // 2
        while d >= 1:
            for i in range(PEER_TOPK):
                if i & d == 0:
                    _compare_exchange(v, i, i + d)
            d //= 2
        shift //= 2
    return v


def _select_top_exact(s, vals_ref):
    n = s.shape[0]
    iota = lax.broadcasted_iota(jnp.int32, s.shape, 0).astype(F32)
    rank = jnp.full(s.shape, NOT_SELECTED, F32)
    for r in range(PEER_TOPK):
        m = jnp.max(s, axis=0, keepdims=True)
        first = jnp.min(jnp.where(s == m, iota, float(n)), axis=0, keepdims=True)
        sel = iota == first
        rank = jnp.where(sel, float(r), rank)
        vals_ref[r:r + 1, :] = m
        s = jnp.where(sel, -jnp.inf, s)
    return rank


def _count(flags):
    return jnp.sum(jnp.where(flags, 1.0, 0.0), axis=0, keepdims=True)


def _top_values(s, vals_ref):
    tm = s.shape[1]
    kk = PEER_TOPK
    s3 = s.reshape(KEY_TILES, SUBLANES, tm)
    v = _sorted_top(s3)
    for r in range(kk):
        vals_ref[r:r + 1, :] = v[r][0:1, :]
    reach = jnp.sum(jnp.sum(jnp.where(s3 >= v[kk - 1][None], 1.0, 0.0), axis=0), axis=0, keepdims=True)
    equal = jnp.zeros((SUBLANES, tm), F32)
    for r in range(kk - 1):
        equal = equal + jnp.where(v[r] == v[r + 1], 1.0, 0.0)
    return v, jnp.abs(reach - kk) + equal[0:1, :]


def _choose_pairs(vals_ref, cand_ref, exact_ties):
    tm = vals_ref.shape[2]
    kk = PEER_TOPK
    cand_ref[...] = jnp.full(cand_ref.shape, -jnp.inf, F32)
    for r1 in range(kk):
        n2, off = CAND_COUNTS[r1], CAND_OFFSETS[r1]
        cand_ref[off:off + n2, :] = vals_ref[0, r1:r1 + 1, :] + vals_ref[1, 0:n2, :]
    start = cand_ref[...]
    cand = start
    iota_c = lax.broadcasted_iota(jnp.int32, cand.shape, 0).astype(F32) if exact_ties else None
    best = vals_ref[0, 0:1, :] + vals_ref[1, 0:1, :]
    z = jnp.zeros((1, tm), F32)
    for r in range(kk):
        m = jnp.max(cand, axis=0, keepdims=True)
        sel = cand == m
        if exact_ties:
            first = jnp.min(jnp.where(sel, iota_c, float(CAND_ROWS)), axis=0, keepdims=True)
            sel = iota_c == first
        z = z + jnp.exp(m - best)
        cand = jnp.where(sel, -jnp.inf, cand)
    chosen = jnp.where(start != cand, 1.0, 0.0)
    return chosen, z, jnp.abs(jnp.sum(chosen, axis=0, keepdims=True) - kk)


def _write_head(h, s1, s2, key1, match, rank2, chosen, z, vals_ref, rank2_ref, e2_ref, cnt_ref, c_ref):
    tm = s1.shape[1]
    key1 = key1.reshape(KEY_TILES, SUBLANES, tm)
    cnt = jnp.zeros(key1.shape, F32)
    for r1 in range(PEER_TOPK):
        n2, off = CAND_COUNTS[r1], CAND_OFFSETS[r1]
        count = jnp.sum(chosen[off:off + n2, :], axis=0, keepdims=True)
        cnt = cnt + jnp.where(key1 == match[r1][None], count[None], 0.0)
    rank2_ref[h] = rank2.astype(BF16)
    cnt_ref[h] = cnt.reshape(PEER_NKEYS, tm)
    c_ref[h] = jnp.exp(s1 - vals_ref[0, 0:1, :]) / z
    e2_ref[h] = jnp.exp(s2 - vals_ref[1, 0:1, :]).astype(BF16)


def _route_kernel(x1_ref, wq_ref, keys_ref, rank2_ref, e2_ref, cnt_ref, c_ref, s_ref, vals_ref, cand_ref, bad_ref):
    tm = x1_ref.shape[0]
    kk = PEER_TOPK
    outs = (rank2_ref, e2_ref, cnt_ref, c_ref)
    q = _dot(x1_ref[...].astype(BF16), wq_ref[...])
    for hp in range(2 * PEER_HEADS):
        qq = q[:, hp * LANES:(hp + 1) * LANES]
        qc = qq - jnp.mean(qq, axis=-1, keepdims=True)
        qn = (qc * lax.rsqrt(jnp.mean(qc * qc, axis=-1, keepdims=True) + LN_EPS)).astype(BF16)
        s_ref[hp] = _dot_nt(keys_ref[hp], qn)

    any_bad = jnp.zeros((1, tm), F32)
    for h in range(PEER_HEADS):
        s1, s2 = s_ref[2 * h], s_ref[2 * h + 1]
        v1, bad1 = _top_values(s1, vals_ref.at[h, 0])
        v2, bad2 = _top_values(s2, vals_ref.at[h, 1])
        s2t = s2.reshape(KEY_TILES, SUBLANES, tm)
        rank2 = jnp.zeros(s2t.shape, F32)
        for r in range(kk):
            rank2 = jnp.where(s2t < v2[r][None], float(r + 1), rank2)
        chosen, z, bad3 = _choose_pairs(vals_ref.at[h], cand_ref.at[h], False)
        _write_head(h, s1, s2, s1, v1, rank2.reshape(PEER_NKEYS, tm), chosen, z, vals_ref.at[h], *outs)
        bad = bad1 + bad2 + bad3
        bad_ref[h:h + 1, :] = bad
        any_bad = jnp.maximum(any_bad, bad)

    @pl.when(jnp.max(any_bad) > 0.0)
    def _():
        def redo(h, carry):
            @pl.when(jnp.max(bad_ref[pl.ds(h, 1), :]) > 0.0)
            def _():
                s1, s2 = s_ref[2 * h], s_ref[2 * h + 1]
                rank1 = _select_top_exact(s1, vals_ref.at[0, 0])
                rank2 = _select_top_exact(s2, vals_ref.at[0, 1])
                chosen, z, _ = _choose_pairs(vals_ref.at[0], cand_ref.at[0], True)
                ranks = [jnp.full((SUBLANES, tm), float(r), F32) for r in range(kk)]
                _write_head(h, s1, s2, rank1, ranks, rank2, chosen, z, vals_ref.at[0], *outs)
            return carry

        lax.fori_loop(0, PEER_HEADS, redo, 0)


def _route(x1, wq, keys, *, tm):
    t, d = x1.shape
    fac = pl.BlockSpec((PEER_HEADS, PEER_NKEYS, tm), lambda i: (0, 0, i))
    shape = lambda dt: jax.ShapeDtypeStruct((PEER_HEADS, PEER_NKEYS, t), dt)
    return pl.pallas_call(
        _route_kernel,
        grid=(t // tm,),
        in_specs=[pl.BlockSpec((tm, d), lambda i: (i, 0)), _const_spec(wq.shape), _const_spec(keys.shape)],
        out_specs=[fac, fac, fac, fac],
        out_shape=[shape(BF16), shape(BF16), shape(F32), shape(F32)],
        scratch_shapes=[pltpu.VMEM((2 * PEER_HEADS, PEER_NKEYS, tm), F32),
                        pltpu.VMEM((PEER_HEADS, 2, PEER_TOPK, tm), F32),
                        pltpu.VMEM((PEER_HEADS, CAND_ROWS, tm), F32),
                        pltpu.VMEM((PEER_HEADS, tm), F32)],
        compiler_params=_params(("arbitrary",)),
        name="peer_route",
    )(x1, wq, keys)


EXPERT_TILE = 1024
GROUP = 2
GATE_COLS = 256
BF16_ROWS = 16


def _experts_kernel(x1t_ref, u_ref, vt_ref, rank2_ref, e2_ref, cnt_ref, c_ref, o_ref, acc_ref, wt_ref, *, na):
    j = pl.program_id(1)
    tm = x1t_ref.shape[1]
    ch = min(GATE_COLS, tm)
    tiles = PEER_NKEYS // BF16_ROWS

    @pl.when(j == 0)
    def _():
        acc_ref[...] = jnp.zeros_like(acc_ref)

    rows = GROUP * PEER_NKEYS

    def group(g, carry):
        r0 = pl.multiple_of(g * rows, rows)
        ht = jax.nn.gelu(_dot(u_ref[pl.ds(r0, rows), :], x1t_ref[...]))
        for k in range(GROUP):
            row = pl.ds(g * GROUP + k, 1)
            for cc in range(tm // ch):
                cols = slice(cc * ch, (cc + 1) * ch)
                gate = None
                for h in range(PEER_HEADS):
                    cnt16 = jnp.broadcast_to(cnt_ref[h, row, cols], (BF16_ROWS, ch)).astype(BF16)
                    c16 = jnp.broadcast_to(c_ref[h, row, cols], (BF16_ROWS, ch)).astype(BF16)
                    r2 = rank2_ref[h, :, cols].reshape(tiles, BF16_ROWS, ch)
                    e2 = e2_ref[h, :, cols].reshape(tiles, BF16_ROWS, ch)
                    term = jnp.where(r2 < cnt16[None], e2 * c16[None], jnp.zeros((), BF16))
                    gate = term if gate is None else gate + term
                hk = ht[k * PEER_NKEYS:(k + 1) * PEER_NKEYS, cols].astype(BF16).reshape(tiles, BF16_ROWS, ch)
                wt_ref[pl.ds(r0 + k * PEER_NKEYS, PEER_NKEYS), cols] = (hk * gate).reshape(PEER_NKEYS, ch)
        return carry

    lax.fori_loop(0, na // GROUP, group, 0)
    acc_ref[...] += _dot(vt_ref[0], wt_ref[...])

    @pl.when(j == pl.num_programs(1) - 1)
    def _():
        o_ref[...] = acc_ref[...].T


def _experts(x1t, u, vt, rank2, e2, cnt, c, *, tm, tn):
    d, t = x1t.shape
    n = u.shape[0]
    na = tn // PEER_NKEYS
    fac = pl.BlockSpec((PEER_HEADS, PEER_NKEYS, tm), lambda i, j: (0, 0, i))
    rows = pl.BlockSpec((PEER_HEADS, na, tm), lambda i, j: (0, j, i))
    return pl.pallas_call(
        functools.partial(_experts_kernel, na=na),
        grid=(t // tm, n // tn),
        in_specs=[pl.BlockSpec((d, tm), lambda i, j: (0, i)),
                  pl.BlockSpec((tn, d), lambda i, j: (j, 0)),
                  pl.BlockSpec((1, d, tn), lambda i, j: (j, 0, 0)),
                  fac, fac, rows, rows],
        out_specs=pl.BlockSpec((tm, d), lambda i, j: (i, 0)),
        out_shape=jax.ShapeDtypeStruct((t, d), F32),
        scratch_shapes=[pltpu.VMEM((d, tm), F32), pltpu.VMEM((tn, tm), BF16)],
        compiler_params=_params(("arbitrary", "arbitrary")),
        name="peer_experts",
    )(x1t, u, vt, rank2, e2, cnt, c)


def _final_kernel(x1_ref, peer_ref, p_ref, g_ref, b_ref, wpe_ref, wpg_ref, o_ref, *, alpha):
    x2 = _layernorm(alpha * x1_ref[...] + peer_ref[...], g_ref[...], b_ref[...])
    emb = _dot(p_ref[...].astype(BF16), wpe_ref[...])
    o_ref[...] = x2 + emb * jax.nn.sigmoid(_dot(x2.astype(BF16), wpg_ref[...]))


def _final(x1, peer, p, w, *, tm, alpha):
    t, d = x1.shape
    row = pl.BlockSpec((tm, d), lambda i: (i, 0))
    weights = [w['ln2g'], w['ln2b'], w['wpe'], w['wpg']]
    return pl.pallas_call(
        functools.partial(_final_kernel, alpha=alpha),
        grid=(t // tm,),
        in_specs=[row, row, pl.BlockSpec((tm, p.shape[1]), lambda i: (i, 0))]
                 + [_const_spec(a.shape) for a in weights],
        out_specs=row,
        out_shape=jax.ShapeDtypeStruct((t, d), F32),
        compiler_params=_params(("arbitrary",)),
        name="ln2_embed",
    )(x1, peer, p, *weights)


def _prep_weights(w_in, w_s, b_s, ln_v_g, ln_v_b, w_gk, b_gk, gla_norm_g, w_br_a, w_br_b, w_o, ln1_g, ln1_b,
                  peer_wq, peer_keys, peer_u, peer_v, ln2_g, ln2_b, w_pe, w_pg):
    d = w_in.shape[0]
    aw = ln_v_g.shape[0]
    dqk = w_gk.shape[1]
    dvt = w_br_b.shape[0]
    rank = w_gk.shape[0]
    sizes = (aw, aw, dqk, dqk, dvt, dvt, rank, d, d)
    offs = [0]
    for s in sizes:
        offs.append(offs[-1] + s)
    cols = lambda i: w_in[:, offs[i]:offs[i + 1]].astype(BF16)
    r2 = lambda a: a.reshape(1, -1)
    gdim = aw // A_GROUPS
    nk = peer_keys.shape[2]
    return {
        'wua': cols(0), 'wva': cols(1), 'wq': cols(2), 'wk': cols(3), 'wv': cols(4), 'wr': cols(5),
        'wgl': jnp.pad(cols(6), ((0, 0), (0, LANES - rank))),
        'wgk': jnp.pad(w_gk.astype(BF16), ((0, LANES - rank), (0, 0))),
        'bgk': r2(b_gk), 'wga': cols(7), 'wgb': cols(8),
        'ws': w_s,
        'bsx': jnp.repeat(b_s.T, gdim, axis=1),
        'ws00': r2(jnp.repeat(w_s[:, 0, 0], gdim)),
        'bs0': r2(jnp.repeat(b_s[:, 0], gdim)),
        'lng': r2(ln_v_g), 'lnb': r2(ln_v_b),
        'wbra': w_br_a.astype(BF16), 'wbrb': w_br_b.astype(BF16), 'wo': w_o.astype(BF16),
        'gamma': r2(gla_norm_g), 'ln1g': r2(ln1_g), 'ln1b': r2(ln1_b),
        'pwq': peer_wq.astype(BF16),
        'keys': peer_keys.astype(BF16).reshape(-1, nk, peer_keys.shape[3]),
        'pu': peer_u.astype(BF16),
        'pvt': peer_v.astype(BF16).reshape(-1, EXPERT_TILE, peer_v.shape[1]).transpose(0, 2, 1),
        'ln2g': r2(ln2_g), 'ln2b': r2(ln2_b), 'wpe': w_pe.astype(BF16), 'wpg': w_pg.astype(BF16),
    }


def _block(t, cap):
    best = LANES
    for m in range(LANES, cap + 1, LANES):
        if t % m == 0:
            best = m
    return best


def _layer(x, p, state, w, *, alpha):
    bsz, seq_len, d = x.shape
    t = bsz * seq_len
    x2d = x.reshape(t, d)
    decode = seq_len == 1
    assert decode or (seq_len % A_CHUNK == 0 and state is None)
    assert t % LANES == 0
    if decode:
        ma, q, k, v, b, sr, sgb, cv = _stage1(x2d, w, decode=True, tm=_block(t, 256), seq_len=1)
        yb, s_new = _gla_step(q, k, b, v, sr, w['gamma'], state)
        cv = cv.reshape(bsz, 1, d)
    else:
        ma, q, k, v, b, sr, sgb, cv = _stage1(x2d, w, decode=False, tm=_block(seq_len, 256), seq_len=seq_len)
        yb, s_new = _gla_prompt(q, k, b, v, sr, w['gamma'], bsz=bsz, seq_len=seq_len)
    x1, x1t = _merge(ma, sgb, yb, x2d, w, tm=_block(t, 512), alpha=alpha)
    rank2, e2, cnt, c = _route(x1, w['pwq'], w['keys'], tm=LANES)
    peer = _experts(x1t, w['pu'], w['pvt'], rank2, e2, cnt, c, tm=_block(t, 1024), tn=EXPERT_TILE)
    x3 = _final(x1, peer, p.reshape(t, -1), w, tm=_block(t, 512), alpha=alpha)
    return x3.reshape(bsz, seq_len, d), s_new, cv


def kernel(x_prompt, x_sample, state_gla, p_prompt, p_sample, w_in, w_s, b_s, ln_v_g, ln_v_b, w_gk, b_gk,
           gla_norm_g, w_br_a, w_br_b, w_o, ln1_g, ln1_b, peer_wq, peer_keys, peer_u, peer_v, ln2_g, ln2_b,
           w_pe, w_pg):
    depth = w_in.shape[0]
    alpha = (2.0 * depth) ** 0.25
    yp, ys = x_prompt, x_sample
    gla_p, gla_s, cv_p, cv_s = [], [], [], []
    for i in range(depth):
        w = _prep_weights(w_in[i], w_s[i], b_s[i], ln_v_g[i], ln_v_b[i], w_gk[i], b_gk[i], gla_norm_g[i],
                          w_br_a[i], w_br_b[i], w_o[i], ln1_g[i], ln1_b[i], peer_wq[i], peer_keys[i],
                          peer_u[i], peer_v[i], ln2_g[i], ln2_b[i], w_pe[i], w_pg[i])
        yp, sp, vp = _layer(yp, p_prompt[i], None, w, alpha=alpha)
        ys, ss, vs = _layer(ys, p_sample[i], state_gla[i], w, alpha=alpha)
        gla_p.append(sp)
        gla_s.append(ss)
        cv_p.append(vp)
        cv_s.append(vs)
    return (yp, ys, jnp.stack(gla_p), jnp.stack(gla_s), jnp.stack(cv_p), jnp.stack(cv_s))
```

```python
import functools
import math

import jax
import jax.numpy as jnp
from jax import lax
from jax.experimental import pallas as pl
from jax.experimental.pallas import tpu as pltpu

F32 = jnp.float32
BF16 = jnp.bfloat16

LANES = 128
A_CHUNK = 128
A_GROUPS = 8
GLA_HEADS = 4
GLA_TAU = 16.0
GLA_CHUNK = 128
PEER_HEADS = 8
PEER_NKEYS = 128
PEER_TOPK = 16
LN_EPS = 1e-5
NOT_SELECTED = 255.0
VMEM_LIMIT = 56 * 1024 * 1024


def _dot(a, b):
    return jnp.dot(a, b, preferred_element_type=F32)


def _dot_nt(a, b):
    return lax.dot_general(a, b, (((1,), (1,)), ((), ())), preferred_element_type=F32)


def _layernorm(x, g, b):
    mu = jnp.mean(x, axis=-1, keepdims=True)
    xc = x - mu
    var = jnp.mean(xc * xc, axis=-1, keepdims=True)
    return xc * lax.rsqrt(var + LN_EPS) * g + b


def _gelu(x):
    k = -2.0 * math.sqrt(2.0 / math.pi) * math.log2(math.e)
    return x / (1.0 + jnp.exp2((x * x * (0.044715 * k) + k) * x))


def _log_sigmoid(z):
    return jnp.minimum(z, 0.0) - jnp.log1p(jnp.exp(-jnp.abs(z)))


def _const_spec(shape):
    nd = len(shape)
    return pl.BlockSpec(shape, lambda *_: (0,) * nd, pipeline_mode=pl.Buffered(1))


def _params(semantics):
    return pltpu.CompilerParams(dimension_semantics=semantics, vmem_limit_bytes=VMEM_LIMIT)


def _stage1_kernel(x_ref, wua_ref, wva_ref, wq_ref, wk_ref, wv_ref, wr_ref, wgl_ref, wgk_ref, bgk_ref,
                   wga_ref, wgb_ref, ws_ref, bs_ref, lng_ref, lnb_ref, wbra_ref,
                   ma_ref, q_ref, k_ref, v_ref, b_ref, sr_ref, sgb_ref, cv_ref, ya_ref, *, decode, tm, dk):
    xb = x_ref[...].astype(BF16)
    gu = _gelu(_dot(xb, wua_ref[...]))
    van = _layernorm(_gelu(_dot(xb, wva_ref[...])), lng_ref[...], lnb_ref[...])
    if decode:
        cv_ref[...] = van
        ya_ref[...] = (gu * (van * ws_ref[...] + bs_ref[...])).astype(BF16)
    else:
        cv_ref[0] = van[tm - A_CHUNK:, :]
        vb = van.astype(BF16)
        row = lax.broadcasted_iota(jnp.int32, (A_CHUNK, A_CHUNK), 0)
        col = lax.broadcasted_iota(jnp.int32, (A_CHUNK, A_CHUNK), 1)
        for g in range(A_GROUPS):
            w = jnp.where(row >= col, ws_ref[g], 0.0).astype(BF16)
            cs = slice(g * LANES, (g + 1) * LANES)
            for c in range(tm // A_CHUNK):
                rs = slice(c * A_CHUNK, (c + 1) * A_CHUNK)
                z = _dot(w, vb[rs, cs]) + bs_ref[:, cs]
                ya_ref[rs, cs] = (gu[rs, cs] * z).astype(BF16)
    ma_ref[...] = jax.nn.sigmoid(_dot(xb, wga_ref[...])) * _dot(ya_ref[...], wbra_ref[...])
    q_ref[...] = _dot(xb, wq_ref[...]) * (dk ** -0.5)
    k_ref[...] = _dot(xb, wk_ref[...])
    v_ref[...] = _dot(xb, wv_ref[...])
    sr_ref[...] = jax.nn.silu(_dot(xb, wr_ref[...]))
    sgb_ref[...] = jax.nn.sigmoid(_dot(xb, wgb_ref[...]))
    gk_low = _dot(xb, wgl_ref[...]).astype(BF16)
    log_a = _log_sigmoid(_dot(gk_low, wgk_ref[...]) + bgk_ref[...]) / GLA_TAU
    if not decode:
        pos = lax.broadcasted_iota(jnp.int32, log_a.shape, 0) % GLA_CHUNK
        shift = 1
        while shift < GLA_CHUNK:
            log_a = log_a + jnp.where(pos >= shift, pltpu.roll(log_a, shift, axis=0), 0.0)
            shift *= 2
    b_ref[...] = log_a


def _stage1(x, w, *, decode, tm, seq_len):
    t, d = x.shape
    dqk = w['wq'].shape[1]
    dv = w['wv'].shape[1]
    steps = t // tm
    row = lambda n: pl.BlockSpec((tm, n), lambda i: (i, 0))
    if decode:
        cv_shape = jax.ShapeDtypeStruct((t, d), F32)
        cv_spec = row(d)
        ws, bs = w['ws00'], w['bs0']
    else:
        per_seq = seq_len // tm
        cv_shape = jax.ShapeDtypeStruct((t // seq_len, A_CHUNK, d), F32)
        cv_spec = pl.BlockSpec((1, A_CHUNK, d), lambda i: (i // per_seq, 0, 0))
        ws, bs = w['ws'], w['bsx']
    weights = [w['wua'], w['wva'], w['wq'], w['wk'], w['wv'], w['wr'], w['wgl'], w['wgk'], w['bgk'],
               w['wga'], w['wgb'], ws, bs, w['lng'], w['lnb'], w['wbra']]
    out_shape = [jax.ShapeDtypeStruct((t, d), F32),
                 jax.ShapeDtypeStruct((t, dqk), F32),
                 jax.ShapeDtypeStruct((t, dqk), F32),
                 jax.ShapeDtypeStruct((t, dv), F32),
                 jax.ShapeDtypeStruct((t, dqk), F32),
                 jax.ShapeDtypeStruct((t, dv), F32),
                 jax.ShapeDtypeStruct((t, d), F32),
                 cv_shape]
    out_specs = [row(d), row(dqk), row(dqk), row(dv), row(dqk), row(dv), row(d), cv_spec]
    return pl.pallas_call(
        functools.partial(_stage1_kernel, decode=decode, tm=tm, dk=dqk // GLA_HEADS),
        grid=(steps,),
        in_specs=[row(d)] + [_const_spec(a.shape) for a in weights],
        out_specs=out_specs,
        out_shape=out_shape,
        scratch_shapes=[pltpu.VMEM((tm, d), BF16)],
        compiler_params=_params(("arbitrary",)),
        name="stage1_decode" if decode else "stage1_prompt",
    )(x, *weights)


def _block_ref_rows(b, w):
    c, n = b.shape
    if w >= 8:
        r = b.reshape(c // (2 * w), 2 * w, n)
        return jnp.broadcast_to(r[:, w - 1:w, :], r.shape).reshape(c, n)
    r = b.reshape(c // 8, 8, n)
    sub = lax.broadcasted_iota(jnp.int32, r.shape, 1)
    pick = lambda j: jnp.broadcast_to(r[:, j:j + 1, :], r.shape)
    if w == 4:
        out = pick(3)
    elif w == 2:
        out = jnp.where(sub < 4, pick(1), pick(5))
    else:
        out = jnp.where(sub < 2, pick(0), jnp.where(sub < 4, pick(2), jnp.where(sub < 6, pick(4), pick(6))))
    return out.reshape(c, n)


def _gla_kernel(q_ref, k_ref, b_ref, v_ref, sr_ref, gam_ref, yb_ref, sfin_ref, s_ref, *, dk, dv):
    c = pl.program_id(1)

    @pl.when(c == 0)
    def _():
        s_ref[...] = jnp.zeros_like(s_ref)

    cc = GLA_CHUNK
    row = lax.broadcasted_iota(jnp.int32, (cc, cc), 0)
    col = lax.broadcasted_iota(jnp.int32, (cc, cc), 1)
    rowk = lax.broadcasted_iota(jnp.int32, (cc, dk), 0)
    for h in range(GLA_HEADS):
        ks = slice(h * dk, (h + 1) * dk)
        vs = slice(h * dv, (h + 1) * dv)
        qh, kh, bh = q_ref[:, ks], k_ref[:, ks], b_ref[:, ks]
        vb = v_ref[:, vs].astype(BF16)
        s_old = s_ref[h]
        o = _dot((qh * jnp.exp(bh)).astype(BF16), s_old.astype(BF16))
        att = jnp.where(row == col, jnp.sum(qh * kh, axis=1, keepdims=True), 0.0)
        w = cc // 2
        while w >= 1:
            bref = _block_ref_rows(bh, w)
            second = (rowk // w) % 2 == 1
            ql = jnp.where(second, qh * jnp.exp(jnp.minimum(bh - bref, 0.0)), 0.0)
            kl = jnp.where(second, 0.0, kh * jnp.exp(jnp.minimum(bref - bh, 0.0)))
            a_l = _dot_nt(ql.astype(BF16), kl.astype(BF16))
            att = att + jnp.where(row // (2 * w) == col // (2 * w), a_l, 0.0)
            w //= 2
        o = o + _dot(att.astype(BF16), vb)
        b_last = bh[cc - 1:cc, :]
        kdec_t = (kh * jnp.exp(b_last - bh)).T.astype(BF16)
        decay_col = jnp.exp(bh.T[:, cc - 1:cc])
        s_new = decay_col * s_old + _dot(kdec_t, vb)
        s_ref[h] = s_new
        o = o * lax.rsqrt(jnp.mean(o * o, axis=-1, keepdims=True) + LN_EPS) * gam_ref[...]
        yb_ref[:, vs] = (sr_ref[:, vs] * o).astype(BF16)

    @pl.when(c == pl.num_programs(1) - 1)
    def _():
        sfin_ref[0] = s_ref[...]


def _gla_prompt(q, k, b, v, sr, gamma, *, bsz, seq_len):
    t, dqk = q.shape
    dvt = v.shape[1]
    dk, dv = dqk // GLA_HEADS, dvt // GLA_HEADS
    nc = seq_len // GLA_CHUNK
    row = lambda n: pl.BlockSpec((GLA_CHUNK, n), lambda i, c: (i * nc + c, 0))
    return pl.pallas_call(
        functools.partial(_gla_kernel, dk=dk, dv=dv),
        grid=(bsz, nc),
        in_specs=[row(dqk), row(dqk), row(dqk), row(dvt), row(dvt), _const_spec(gamma.shape)],
        out_specs=[row(dvt), pl.BlockSpec((1, GLA_HEADS, dk, dv), lambda i, c: (i, 0, 0, 0))],
        out_shape=[jax.ShapeDtypeStruct((t, dvt), BF16),
                   jax.ShapeDtypeStruct((bsz, GLA_HEADS, dk, dv), F32)],
        scratch_shapes=[pltpu.VMEM((GLA_HEADS, dk, dv), F32)],
        compiler_params=_params(("arbitrary", "arbitrary")),
        name="gla_prompt",
    )(q, k, b, v, sr, gamma)


DEC_TOKENS = 16


def _gla_step_kernel(qt_ref, kt_ref, gt_ref, v_ref, sr_ref, gam_ref, s_ref, yb_ref, sout_ref, *, dk, dv):
    for j in range(DEC_TOKENS):
        for h in range(GLA_HEADS):
            ks = slice(h * dk, (h + 1) * dk)
            vs = slice(h * dv, (h + 1) * dv)
            decay = jnp.exp(gt_ref[0, ks, j:j + 1])
            s_new = decay * s_ref[j, h] + kt_ref[0, ks, j:j + 1] * v_ref[j:j + 1, vs]
            sout_ref[j, h] = s_new
            o = jnp.sum(qt_ref[0, ks, j:j + 1] * s_new, axis=0, keepdims=True)
            o = o * lax.rsqrt(jnp.mean(o * o, axis=-1, keepdims=True) + LN_EPS) * gam_ref[...]
            yb_ref[j:j + 1, vs] = (sr_ref[j:j + 1, vs] * o).astype(BF16)


def _gla_step(q, k, g, v, sr, gamma, state):
    t, dqk = q.shape
    dvt = v.shape[1]
    dk, dv = dqk // GLA_HEADS, dvt // GLA_HEADS
    steps = t // DEC_TOKENS
    tr = lambda a: a.reshape(steps, DEC_TOKENS, dqk).transpose(0, 2, 1)
    col = pl.BlockSpec((1, dqk, DEC_TOKENS), lambda i: (i, 0, 0))
    row = lambda n: pl.BlockSpec((DEC_TOKENS, n), lambda i: (i, 0))
    st = pl.BlockSpec((DEC_TOKENS, GLA_HEADS, dk, dv), lambda i: (i, 0, 0, 0))
    return pl.pallas_call(
        functools.partial(_gla_step_kernel, dk=dk, dv=dv),
        grid=(steps,),
        in_specs=[col, col, col, row(dvt), row(dvt), _const_spec(gamma.shape), st],
        out_specs=[row(dvt), st],
        out_shape=[jax.ShapeDtypeStruct((t, dvt), BF16), jax.ShapeDtypeStruct(state.shape, F32)],
        compiler_params=_params(("arbitrary",)),
        name="gla_step",
    )(tr(q), tr(k), tr(g), v, sr, gamma, state)


def _merge_kernel(ma_ref, sgb_ref, yb_ref, x_ref, wbrb_ref, wo_ref, g_ref, b_ref, x1_ref, x1t_ref, *, alpha):
    m = ma_ref[...] + sgb_ref[...] * _dot(yb_ref[...], wbrb_ref[...])
    x1 = _layernorm(alpha * x_ref[...] + _dot(m.astype(BF16), wo_ref[...]), g_ref[...], b_ref[...])
    x1_ref[...] = x1
    x1t_ref[...] = x1.T.astype(BF16)


def _merge(ma, sgb, yb, x, w, *, tm, alpha):
    t, d = x.shape
    row = pl.BlockSpec((tm, d), lambda i: (i, 0))
    weights = [w['wbrb'], w['wo'], w['ln1g'], w['ln1b']]
    return pl.pallas_call(
        functools.partial(_merge_kernel, alpha=alpha),
        grid=(t // tm,),
        in_specs=[row, row, row, row] + [_const_spec(a.shape) for a in weights],
        out_specs=[row, pl.BlockSpec((d, tm), lambda i: (0, i))],
        out_shape=[jax.ShapeDtypeStruct((t, d), F32), jax.ShapeDtypeStruct((d, t), BF16)],
        compiler_params=_params(("arbitrary",)),
        name="merge_ln1",
    )(ma, sgb, yb, x, *weights)


CAND_COUNTS = tuple(PEER_TOPK // (r1 + 1) for r1 in range(PEER_TOPK))
CAND_OFFSETS = tuple(sum(CAND_COUNTS[:r1]) for r1 in range(PEER_TOPK))
CAND_ROWS = -(-sum(CAND_COUNTS) // 8) * 8


SUBLANES = 8
KEY_TILES = PEER_NKEYS // SUBLANES
assert KEY_TILES == PEER_TOPK


def _odd_even_merge(lo, hi, r):
    step = r * 2
    if step < hi - lo:
        yield from _odd_even_merge(lo, hi, step)
        yield from _odd_even_merge(lo + r, hi, step)
        yield from [(i, i + r) for i in range(lo + r, hi - r, step)]
    else:
        yield (lo, lo + r)


def _odd_even_merge_sort(lo, hi):
    if hi - lo >= 1:
        mid = lo + (hi - lo) // 2
        yield from _odd_even_merge_sort(lo, mid)
        yield from _odd_even_merge_sort(mid + 1, hi)
        yield from _odd_even_merge(lo, hi, 1)


SORT_NET = tuple(_odd_even_merge_sort(0, KEY_TILES - 1))


def _compare_exchange(v, i, j):
    v[i], v[j] = jnp.maximum(v[i], v[j]), jnp.minimum(v[i], v[j])


def _sorted_top(s3):
    v = [s3[i] for i in range(KEY_TILES)]
    for i, j in SORT_NET:
        _compare_exchange(v, i, j)
    shift = SUBLANES // 2
    while shift >= 1:
        other = [pltpu.roll(x, shift, axis=0) for x in v]
        v = [jnp.maximum(v[i], other[PEER_TOPK - 1 - i]) for i in range(PEER_TOPK)]
        d = PEER_TOPK // 2
        while d >= 1:
            for i in range(PEER_TOPK):
                if i & d == 0:
                    _compare_exchange(v, i, i + d)
            d //= 2
        shift //= 2
    return v


def _select_top_exact(s, vals_ref):
    n = s.shape[0]
    iota = lax.broadcasted_iota(jnp.int32, s.shape, 0).astype(F32)
    rank = jnp.full(s.shape, NOT_SELECTED, F32)
    for r in range(PEER_TOPK):
        m = jnp.max(s, axis=0, keepdims=True)
        first = jnp.min(jnp.where(s == m, iota, float(n)), axis=0, keepdims=True)
        sel = iota == first
        rank = jnp.where(sel, float(r), rank)
        vals_ref[r:r + 1, :] = m
        s = jnp.where(sel, -jnp.inf, s)
    return rank


def _count(flags):
    return jnp.sum(jnp.where(flags, 1.0, 0.0), axis=0, keepdims=True)


def _top_values(s, vals_ref):
    tm = s.shape[1]
    kk = PEER_TOPK
    s3 = s.reshape(KEY_TILES, SUBLANES, tm)
    v = _sorted_top(s3)
    for r in range(kk):
        vals_ref[r:r + 1, :] = v[r][0:1, :]
    reach = jnp.sum(jnp.sum(jnp.where(s3 >= v[kk - 1][None], 1.0, 0.0), axis=0), axis=0, keepdims=True)
    equal = jnp.zeros((SUBLANES, tm), F32)
    for r in range(kk - 1):
        equal = equal + jnp.where(v[r] == v[r + 1], 1.0, 0.0)
    return v, jnp.abs(reach - kk) + equal[0:1, :]


def _choose_pairs(vals_ref, cand_ref, exact_ties):
    tm = vals_ref.shape[2]
    kk = PEER_TOPK
    cand_ref[...] = jnp.full(cand_ref.shape, -jnp.inf, F32)
    for r1 in range(kk):
        n2, off = CAND_COUNTS[r1], CAND_OFFSETS[r1]
        cand_ref[off:off + n2, :] = vals_ref[0, r1:r1 + 1, :] + vals_ref[1, 0:n2, :]
    start = cand_ref[...]
    cand = start
    iota_c = lax.broadcasted_iota(jnp.int32, cand.shape, 0).astype(F32) if exact_ties else None
    best = vals_ref[0, 0:1, :] + vals_ref[1, 0:1, :]
    z = jnp.zeros((1, tm), F32)
    for r in range(kk):
        m = jnp.max(cand, axis=0, keepdims=True)
        sel = cand == m
        if exact_ties:
            first = jnp.min(jnp.where(sel, iota_c, float(CAND_ROWS)), axis=0, keepdims=True)
            sel = iota_c == first
        z = z + jnp.exp(m - best)
        cand = jnp.where(sel, -jnp.inf, cand)
    chosen = jnp.where(start != cand, 1.0, 0.0)
    return chosen, z, jnp.abs(jnp.sum(chosen, axis=0, keepdims=True) - kk)


def _write_head(h, s1, s2, key1, match, rank2, chosen, z, vals_ref, rank2_ref, e2_ref, cnt_ref, c_ref):
    tm = s1.shape[1]
    key1 = key1.reshape(KEY_TILES, SUBLANES, tm)
    cnt = jnp.zeros(key1.shape, F32)
    for r1 in range(PEER_TOPK):
        n2, off = CAND_COUNTS[r1], CAND_OFFSETS[r1]
        count = jnp.sum(chosen[off:off + n2, :], axis=0, keepdims=True)
        cnt = cnt + jnp.where(key1 == match[r1][None], count[None], 0.0)
    rank2_ref[h] = rank2.astype(BF16)
    cnt_ref[h] = cnt.reshape(PEER_NKEYS, tm)
    c_ref[h] = jnp.exp(s1 - vals_ref[0, 0:1, :]) / z
    e2_ref[h] = jnp.exp(s2 - vals_ref[1, 0:1, :]).astype(BF16)


def _route_kernel(x1_ref, wq_ref, keys_ref, rank2_ref, e2_ref, cnt_ref, c_ref, s_ref, vals_ref, cand_ref, bad_ref):
    tm = x1_ref.shape[0]
    kk = PEER_TOPK
    outs = (rank2_ref, e2_ref, cnt_ref, c_ref)
    q = _dot(x1_ref[...].astype(BF16), wq_ref[...])
    for hp in range(2 * PEER_HEADS):
        qq = q[:, hp * LANES:(hp + 1) * LANES]
        qc = qq - jnp.mean(qq, axis=-1, keepdims=True)
        qn = (qc * lax.rsqrt(jnp.mean(qc * qc, axis=-1, keepdims=True) + LN_EPS)).astype(BF16)
        s_ref[hp] = _dot_nt(keys_ref[hp], qn)

    any_bad = jnp.zeros((1, tm), F32)
    for h in range(PEER_HEADS):
        s1, s2 = s_ref[2 * h], s_ref[2 * h + 1]
        v1, bad1 = _top_values(s1, vals_ref.at[h, 0])
        v2, bad2 = _top_values(s2, vals_ref.at[h, 1])
        s2t = s2.reshape(KEY_TILES, SUBLANES, tm)
        rank2 = jnp.zeros(s2t.shape, F32)
        for r in range(kk):
            rank2 = jnp.where(s2t < v2[r][None], float(r + 1), rank2)
        chosen, z, bad3 = _choose_pairs(vals_ref.at[h], cand_ref.at[h], False)
        _write_head(h, s1, s2, s1, v1, rank2.reshape(PEER_NKEYS, tm), chosen, z, vals_ref.at[h], *outs)
        bad = bad1 + bad2 + bad3
        bad_ref[h:h + 1, :] = bad
        any_bad = jnp.maximum(any_bad, bad)

    @pl.when(jnp.max(any_bad) > 0.0)
    def _():
        def redo(h, carry):
            @pl.when(jnp.max(bad_ref[pl.ds(h, 1), :]) > 0.0)
            def _():
                s1, s2 = s_ref[2 * h], s_ref[2 * h + 1]
                rank1 = _select_top_exact(s1, vals_ref.at[0, 0])
                rank2 = _select_top_exact(s2, vals_ref.at[0, 1])
                chosen, z, _ = _choose_pairs(vals_ref.at[0], cand_ref.at[0], True)
                ranks = [jnp.full((SUBLANES, tm), float(r), F32) for r in range(kk)]
                _write_head(h, s1, s2, rank1, ranks, rank2, chosen, z, vals_ref.at[0], *outs)
            return carry

        lax.fori_loop(0, PEER_HEADS, redo, 0)


def _route(x1, wq, keys, *, tm):
    t, d = x1.shape
    fac = pl.BlockSpec((PEER_HEADS, PEER_NKEYS, tm), lambda i: (0, 0, i))
    shape = lambda dt: jax.ShapeDtypeStruct((PEER_HEADS, PEER_NKEYS, t), dt)
    return pl.pallas_call(
        _route_kernel,
        grid=(t // tm,),
        in_specs=[pl.BlockSpec((tm, d), lambda i: (i, 0)), _const_spec(wq.shape), _const_spec(keys.shape)],
        out_specs=[fac, fac, fac, fac],
        out_shape=[shape(BF16), shape(BF16), shape(F32), shape(F32)],
        scratch_shapes=[pltpu.VMEM((2 * PEER_HEADS, PEER_NKEYS, tm), F32),
                        pltpu.VMEM((PEER_HEADS, 2, PEER_TOPK, tm), F32),
                        pltpu.VMEM((PEER_HEADS, CAND_ROWS, tm), F32),
                        pltpu.VMEM((PEER_HEADS, tm), F32)],
        compiler_params=_params(("arbitrary",)),
        name="peer_route",
    )(x1, wq, keys)


EXPERT_TILE = 1024
GROUP = 2
GATE_COLS = 256
BF16_ROWS = 16


def _experts_kernel(x1t_ref, u_ref, vt_ref, rank2_ref, e2_ref, cnt_ref, c_ref, o_ref, acc_ref, wt_ref, *, na):
    j = pl.program_id(1)
    tm = x1t_ref.shape[1]
    ch = min(GATE_COLS, tm)
    tiles = PEER_NKEYS // BF16_ROWS

    @pl.when(j == 0)
    def _():
        acc_ref[...] = jnp.zeros_like(acc_ref)

    rows = GROUP * PEER_NKEYS

    def group(g, carry):
        r0 = pl.multiple_of(g * rows, rows)
        ht = _dot(u_ref[pl.ds(r0, rows), :], x1t_ref[...])
        for k in range(GROUP):
            row = pl.ds(g * GROUP + k, 1)
            for cc in range(tm // ch):
                cols = slice(cc * ch, (cc + 1) * ch)
                gate = None
                for h in range(PEER_HEADS):
                    cnt16 = jnp.broadcast_to(cnt_ref[h, row, cols], (BF16_ROWS, ch)).astype(BF16)
                    c16 = jnp.broadcast_to(c_ref[h, row, cols], (BF16_ROWS, ch)).astype(BF16)
                    r2 = rank2_ref[h, :, cols].reshape(tiles, BF16_ROWS, ch)
                    e2 = e2_ref[h, :, cols].reshape(tiles, BF16_ROWS, ch)
                    term = jnp.where(r2 < cnt16[None], e2 * c16[None], jnp.zeros((), BF16))
                    gate = term if gate is None else gate + term
                hk = _gelu(ht[k * PEER_NKEYS:(k + 1) * PEER_NKEYS, cols]).astype(BF16).reshape(tiles, BF16_ROWS, ch)
                wt_ref[pl.ds(r0 + k * PEER_NKEYS, PEER_NKEYS), cols] = (hk * gate).reshape(PEER_NKEYS, ch)
        return carry

    lax.fori_loop(0, na // GROUP, group, 0)
    acc_ref[...] += _dot(vt_ref[0], wt_ref[...])

    @pl.when(j == pl.num_programs(1) - 1)
    def _():
        o_ref[...] = acc_ref[...].T


def _experts(x1t, u, vt, rank2, e2, cnt, c, *, tm, tn):
    d, t = x1t.shape
    n = u.shape[0]
    na = tn // PEER_NKEYS
    fac = pl.BlockSpec((PEER_HEADS, PEER_NKEYS, tm), lambda i, j: (0, 0, i))
    rows = pl.BlockSpec((PEER_HEADS, na, tm), lambda i, j: (0, j, i))
    return pl.pallas_call(
        functools.partial(_experts_kernel, na=na),
        grid=(t // tm, n // tn),
        in_specs=[pl.BlockSpec((d, tm), lambda i, j: (0, i)),
                  pl.BlockSpec((tn, d), lambda i, j: (j, 0)),
                  pl.BlockSpec((1, d, tn), lambda i, j: (j, 0, 0)),
                  fac, fac, rows, rows],
        out_specs=pl.BlockSpec((tm, d), lambda i, j: (i, 0)),
        out_shape=jax.ShapeDtypeStruct((t, d), F32),
        scratch_shapes=[pltpu.VMEM((d, tm), F32), pltpu.VMEM((tn, tm), BF16)],
        compiler_params=_params(("arbitrary", "arbitrary")),
        name="peer_experts",
    )(x1t, u, vt, rank2, e2, cnt, c)


def _final_kernel(x1_ref, peer_ref, p_ref, g_ref, b_ref, wpe_ref, wpg_ref, o_ref, *, alpha):
    x2 = _layernorm(alpha * x1_ref[...] + peer_ref[...], g_ref[...], b_ref[...])
    emb = _dot(p_ref[...].astype(BF16), wpe_ref[...])
    o_ref[...] = x2 + emb * jax.nn.sigmoid(_dot(x2.astype(BF16), wpg_ref[...]))


def _final(x1, peer, p, w, *, tm, alpha):
    t, d = x1.shape
    row = pl.BlockSpec((tm, d), lambda i: (i, 0))
    weights = [w['ln2g'], w['ln2b'], w['wpe'], w['wpg']]
    return pl.pallas_call(
        functools.partial(_final_kernel, alpha=alpha),
        grid=(t // tm,),
        in_specs=[row, row, pl.BlockSpec((tm, p.shape[1]), lambda i: (i, 0))]
                 + [_const_spec(a.shape) for a in weights],
        out_specs=row,
        out_shape=jax.ShapeDtypeStruct((t, d), F32),
        compiler_params=_params(("arbitrary",)),
        name="ln2_embed",
    )(x1, peer, p, *weights)


def _prep_weights(w_in, w_s, b_s, ln_v_g, ln_v_b, w_gk, b_gk, gla_norm_g, w_br_a, w_br_b, w_o, ln1_g, ln1_b,
                  peer_wq, peer_keys, peer_u, peer_v, ln2_g, ln2_b, w_pe, w_pg):
    d = w_in.shape[0]
    aw = ln_v_g.shape[0]
    dqk = w_gk.shape[1]
    dvt = w_br_b.shape[0]
    rank = w_gk.shape[0]
    sizes = (aw, aw, dqk, dqk, dvt, dvt, rank, d, d)
    offs = [0]
    for s in sizes:
        offs.append(offs[-1] + s)
    cols = lambda i: w_in[:, offs[i]:offs[i + 1]].astype(BF16)
    r2 = lambda a: a.reshape(1, -1)
    gdim = aw // A_GROUPS
    nk = peer_keys.shape[2]
    return {
        'wua': cols(0), 'wva': cols(1), 'wq': cols(2), 'wk': cols(3), 'wv': cols(4), 'wr': cols(5),
        'wgl': jnp.pad(cols(6), ((0, 0), (0, LANES - rank))),
        'wgk': jnp.pad(w_gk.astype(BF16), ((0, LANES - rank), (0, 0))),
        'bgk': r2(b_gk), 'wga': cols(7), 'wgb': cols(8),
        'ws': w_s,
        'bsx': jnp.repeat(b_s.T, gdim, axis=1),
        'ws00': r2(jnp.repeat(w_s[:, 0, 0], gdim)),
        'bs0': r2(jnp.repeat(b_s[:, 0], gdim)),
        'lng': r2(ln_v_g), 'lnb': r2(ln_v_b),
        'wbra': w_br_a.astype(BF16), 'wbrb': w_br_b.astype(BF16), 'wo': w_o.astype(BF16),
        'gamma': r2(gla_norm_g), 'ln1g': r2(ln1_g), 'ln1b': r2(ln1_b),
        'pwq': peer_wq.astype(BF16),
        'keys': peer_keys.astype(BF16).reshape(-1, nk, peer_keys.shape[3]),
        'pu': peer_u.astype(BF16),
        'pvt': peer_v.astype(BF16).reshape(-1, EXPERT_TILE, peer_v.shape[1]).transpose(0, 2, 1),
        'ln2g': r2(ln2_g), 'ln2b': r2(ln2_b), 'wpe': w_pe.astype(BF16), 'wpg': w_pg.astype(BF16),
    }


def _block(t, cap):
    best = LANES
    for m in range(LANES, cap + 1, LANES):
        if t % m == 0:
            best = m
    return best


def _layer(x, p, state, w, *, alpha):
    bsz, seq_len, d = x.shape
    t = bsz * seq_len
    x2d = x.reshape(t, d)
    decode = seq_len == 1
    assert decode or (seq_len % A_CHUNK == 0 and state is None)
    assert t % LANES == 0
    if decode:
        ma, q, k, v, b, sr, sgb, cv = _stage1(x2d, w, decode=True, tm=_block(t, 256), seq_len=1)
        yb, s_new = _gla_step(q, k, b, v, sr, w['gamma'], state)
        cv = cv.reshape(bsz, 1, d)
    else:
        ma, q, k, v, b, sr, sgb, cv = _stage1(x2d, w, decode=False, tm=_block(seq_len, 256), seq_len=seq_len)
        yb, s_new = _gla_prompt(q, k, b, v, sr, w['gamma'], bsz=bsz, seq_len=seq_len)
    x1, x1t = _merge(ma, sgb, yb, x2d, w, tm=_block(t, 512), alpha=alpha)
    rank2, e2, cnt, c = _route(x1, w['pwq'], w['keys'], tm=LANES)
    peer = _experts(x1t, w['pu'], w['pvt'], rank2, e2, cnt, c, tm=_block(t, 1024), tn=EXPERT_TILE)
    x3 = _final(x1, peer, p.reshape(t, -1), w, tm=_block(t, 512), alpha=alpha)
    return x3.reshape(bsz, seq_len, d), s_new, cv


def kernel(x_prompt, x_sample, state_gla, p_prompt, p_sample, w_in, w_s, b_s, ln_v_g, ln_v_b, w_gk, b_gk,
           gla_norm_g, w_br_a, w_br_b, w_o, ln1_g, ln1_b, peer_wq, peer_keys, peer_u, peer_v, ln2_g, ln2_b,
           w_pe, w_pg):
    depth = w_in.shape[0]
    alpha = (2.0 * depth) ** 0.25
    yp, ys = x_prompt, x_sample
    gla_p, gla_s, cv_p, cv_s = [], [], [], []
    for i in range(depth):
        w = _prep_weights(w_in[i], w_s[i], b_s[i], ln_v_g[i], ln_v_b[i], w_gk[i], b_gk[i], gla_norm_g[i],
                          w_br_a[i], w_br_b[i], w_o[i], ln1_g[i], ln1_b[i], peer_wq[i], peer_keys[i],
                          peer_u[i], peer_v[i], ln2_g[i], ln2_b[i], w_pe[i], w_pg[i])
        yp, sp, vp = _layer(yp, p_prompt[i], None, w, alpha=alpha)
        ys, ss, vs = _layer(ys, p_sample[i], state_gla[i], w, alpha=alpha)
        gla_p.append(sp)
        gla_s.append(ss)
        cv_p.append(vp)
        cv_s.append(vs)
    return (yp, ys, jnp.stack(gla_p), jnp.stack(gla_s), jnp.stack(cv_p), jnp.stack(cv_s))
```

```python
import functools
import math

import jax
import jax.numpy as jnp
from jax import lax
from jax.experimental import pallas as pl
from jax.experimental.pallas import tpu as pltpu

F32 = jnp.float32
BF16 = jnp.bfloat16

LANES = 128
A_CHUNK = 128
A_GROUPS = 8
GLA_HEADS = 4
GLA_TAU = 16.0
GLA_CHUNK = 128
PEER_HEADS = 8
PEER_NKEYS = 128
PEER_TOPK = 16
LN_EPS = 1e-5
NOT_SELECTED = 255.0
VMEM_LIMIT = 56 * 1024 * 1024


def _dot(a, b):
    return jnp.dot(a, b, preferred_element_type=F32)


def _dot_nt(a, b):
    return lax.dot_general(a, b, (((1,), (1,)), ((), ())), preferred_element_type=F32)


def _layernorm(x, g, b):
    mu = jnp.mean(x, axis=-1, keepdims=True)
    xc = x - mu
    var = jnp.mean(xc * xc, axis=-1, keepdims=True)
    return xc * lax.rsqrt(var + LN_EPS) * g + b


def _gelu(x):
    k = -2.0 * math.sqrt(2.0 / math.pi) * math.log2(math.e)
    return x / (1.0 + jnp.exp2((x * x * (0.044715 * k) + k) * x))


def _log_sigmoid(z):
    return jnp.minimum(z, 0.0) - jnp.log1p(jnp.exp(-jnp.abs(z)))


def _const_spec(shape):
    nd = len(shape)
    return pl.BlockSpec(shape, lambda *_: (0,) * nd, pipeline_mode=pl.Buffered(1))


def _params(semantics):
    return pltpu.CompilerParams(dimension_semantics=semantics, vmem_limit_bytes=VMEM_LIMIT)


def _stage1_kernel(x_ref, wua_ref, wva_ref, wq_ref, wk_ref, wv_ref, wr_ref, wgl_ref, wgk_ref, bgk_ref,
                   wga_ref, wgb_ref, ws_ref, bs_ref, lng_ref, lnb_ref, wbra_ref,
                   ma_ref, q_ref, k_ref, v_ref, b_ref, sr_ref, sgb_ref, cv_ref, ya_ref, *, decode, tm, dk):
    xb = x_ref[...].astype(BF16)
    gu = _gelu(_dot(xb, wua_ref[...]))
    van = _layernorm(_gelu(_dot(xb, wva_ref[...])), lng_ref[...], lnb_ref[...])
    if decode:
        cv_ref[...] = van
        ya_ref[...] = (gu * (van * ws_ref[...] + bs_ref[...])).astype(BF16)
    else:
        cv_ref[0] = van[tm - A_CHUNK:, :]
        vb = van.astype(BF16)
        row = lax.broadcasted_iota(jnp.int32, (A_CHUNK, A_CHUNK), 0)
        col = lax.broadcasted_iota(jnp.int32, (A_CHUNK, A_CHUNK), 1)
        for g in range(A_GROUPS):
            w = jnp.where(row >= col, ws_ref[g], 0.0).astype(BF16)
            cs = slice(g * LANES, (g + 1) * LANES)
            for c in range(tm // A_CHUNK):
                rs = slice(c * A_CHUNK, (c + 1) * A_CHUNK)
                z = _dot(w, vb[rs, cs]) + bs_ref[:, cs]
                ya_ref[rs, cs] = (gu[rs, cs] * z).astype(BF16)
    ma_ref[...] = jax.nn.sigmoid(_dot(xb, wga_ref[...])) * _dot(ya_ref[...], wbra_ref[...])
    q_ref[...] = _dot(xb, wq_ref[...]) * (dk ** -0.5)
    k_ref[...] = _dot(xb, wk_ref[...])
    v_ref[...] = _dot(xb, wv_ref[...])
    sr_ref[...] = jax.nn.silu(_dot(xb, wr_ref[...]))
    sgb_ref[...] = jax.nn.sigmoid(_dot(xb, wgb_ref[...]))
    gk_low = _dot(xb, wgl_ref[...]).astype(BF16)
    log_a = _log_sigmoid(_dot(gk_low, wgk_ref[...]) + bgk_ref[...]) / GLA_TAU
    if not decode:
        pos = lax.broadcasted_iota(jnp.int32, log_a.shape, 0) % GLA_CHUNK
        shift = 1
        while shift < GLA_CHUNK:
            log_a = log_a + jnp.where(pos >= shift, pltpu.roll(log_a, shift, axis=0), 0.0)
            shift *= 2
    b_ref[...] = log_a


def _stage1(x, w, *, decode, tm, seq_len):
    t, d = x.shape
    dqk = w['wq'].shape[1]
    dv = w['wv'].shape[1]
    steps = t // tm
    row = lambda n: pl.BlockSpec((tm, n), lambda i: (i, 0))
    if decode:
        cv_shape = jax.ShapeDtypeStruct((t, d), F32)
        cv_spec = row(d)
        ws, bs = w['ws00'], w['bs0']
    else:
        per_seq = seq_len // tm
        cv_shape = jax.ShapeDtypeStruct((t // seq_len, A_CHUNK, d), F32)
        cv_spec = pl.BlockSpec((1, A_CHUNK, d), lambda i: (i // per_seq, 0, 0))
        ws, bs = w['ws'], w['bsx']
    weights = [w['wua'], w['wva'], w['wq'], w['wk'], w['wv'], w['wr'], w['wgl'], w['wgk'], w['bgk'],
               w['wga'], w['wgb'], ws, bs, w['lng'], w['lnb'], w['wbra']]
    out_shape = [jax.ShapeDtypeStruct((t, d), F32),
                 jax.ShapeDtypeStruct((t, dqk), F32),
                 jax.ShapeDtypeStruct((t, dqk), F32),
                 jax.ShapeDtypeStruct((t, dv), F32),
                 jax.ShapeDtypeStruct((t, dqk), F32),
                 jax.ShapeDtypeStruct((t, dv), F32),
                 jax.ShapeDtypeStruct((t, d), F32),
                 cv_shape]
    out_specs = [row(d), row(dqk), row(dqk), row(dv), row(dqk), row(dv), row(d), cv_spec]
    return pl.pallas_call(
        functools.partial(_stage1_kernel, decode=decode, tm=tm, dk=dqk // GLA_HEADS),
        grid=(steps,),
        in_specs=[row(d)] + [_const_spec(a.shape) for a in weights],
        out_specs=out_specs,
        out_shape=out_shape,
        scratch_shapes=[pltpu.VMEM((tm, d), BF16)],
        compiler_params=_params(("arbitrary",)),
        name="stage1_decode" if decode else "stage1_prompt",
    )(x, *weights)


def _block_ref_rows(b, w):
    c, n = b.shape
    if w >= 8:
        r = b.reshape(c // (2 * w), 2 * w, n)
        return jnp.broadcast_to(r[:, w - 1:w, :], r.shape).reshape(c, n)
    r = b.reshape(c // 8, 8, n)
    sub = lax.broadcasted_iota(jnp.int32, r.shape, 1)
    pick = lambda j: jnp.broadcast_to(r[:, j:j + 1, :], r.shape)
    if w == 4:
        out = pick(3)
    elif w == 2:
        out = jnp.where(sub < 4, pick(1), pick(5))
    else:
        out = jnp.where(sub < 2, pick(0), jnp.where(sub < 4, pick(2), jnp.where(sub < 6, pick(4), pick(6))))
    return out.reshape(c, n)


def _gla_kernel(q_ref, k_ref, b_ref, v_ref, sr_ref, gam_ref, yb_ref, sfin_ref, s_ref, *, dk, dv):
    c = pl.program_id(1)

    @pl.when(c == 0)
    def _():
        s_ref[...] = jnp.zeros_like(s_ref)

    cc = GLA_CHUNK
    row = lax.broadcasted_iota(jnp.int32, (cc, cc), 0)
    col = lax.broadcasted_iota(jnp.int32, (cc, cc), 1)
    rowk = lax.broadcasted_iota(jnp.int32, (cc, dk), 0)
    for h in range(GLA_HEADS):
        ks = slice(h * dk, (h + 1) * dk)
        vs = slice(h * dv, (h + 1) * dv)
        qh, kh, bh = q_ref[:, ks], k_ref[:, ks], b_ref[:, ks]
        vb = v_ref[:, vs].astype(BF16)
        s_old = s_ref[h]
        o = _dot((qh * jnp.exp(bh)).astype(BF16), s_old.astype(BF16))
        att = jnp.where(row == col, jnp.sum(qh * kh, axis=1, keepdims=True), 0.0)
        w = cc // 2
        while w >= 1:
            bref = _block_ref_rows(bh, w)
            second = (rowk // w) % 2 == 1
            ql = jnp.where(second, qh * jnp.exp(jnp.minimum(bh - bref, 0.0)), 0.0)
            kl = jnp.where(second, 0.0, kh * jnp.exp(jnp.minimum(bref - bh, 0.0)))
            a_l = _dot_nt(ql.astype(BF16), kl.astype(BF16))
            att = att + jnp.where(row // (2 * w) == col // (2 * w), a_l, 0.0)
            w //= 2
        o = o + _dot(att.astype(BF16), vb)
        b_last = bh[cc - 1:cc, :]
        kdec_t = (kh * jnp.exp(b_last - bh)).T.astype(BF16)
        decay_col = jnp.exp(bh.T[:, cc - 1:cc])
        s_new = decay_col * s_old + _dot(kdec_t, vb)
        s_ref[h] = s_new
        o = o * lax.rsqrt(jnp.mean(o * o, axis=-1, keepdims=True) + LN_EPS) * gam_ref[...]
        yb_ref[:, vs] = (sr_ref[:, vs] * o).astype(BF16)

    @pl.when(c == pl.num_programs(1) - 1)
    def _():
        sfin_ref[0] = s_ref[...]


def _gla_prompt(q, k, b, v, sr, gamma, *, bsz, seq_len):
    t, dqk = q.shape
    dvt = v.shape[1]
    dk, dv = dqk // GLA_HEADS, dvt // GLA_HEADS
    nc = seq_len // GLA_CHUNK
    row = lambda n: pl.BlockSpec((GLA_CHUNK, n), lambda i, c: (i * nc + c, 0))
    return pl.pallas_call(
        functools.partial(_gla_kernel, dk=dk, dv=dv),
        grid=(bsz, nc),
        in_specs=[row(dqk), row(dqk), row(dqk), row(dvt), row(dvt), _const_spec(gamma.shape)],
        out_specs=[row(dvt), pl.BlockSpec((1, GLA_HEADS, dk, dv), lambda i, c: (i, 0, 0, 0))],
        out_shape=[jax.ShapeDtypeStruct((t, dvt), BF16),
                   jax.ShapeDtypeStruct((bsz, GLA_HEADS, dk, dv), F32)],
        scratch_shapes=[pltpu.VMEM((GLA_HEADS, dk, dv), F32)],
        compiler_params=_params(("arbitrary", "arbitrary")),
        name="gla_prompt",
    )(q, k, b, v, sr, gamma)


DEC_TOKENS = 16


def _gla_step_kernel(qt_ref, kt_ref, gt_ref, v_ref, sr_ref, gam_ref, s_ref, yb_ref, sout_ref, *, dk, dv):
    for j in range(DEC_TOKENS):
        for h in range(GLA_HEADS):
            ks = slice(h * dk, (h + 1) * dk)
            vs = slice(h * dv, (h + 1) * dv)
            decay = jnp.exp(gt_ref[0, ks, j:j + 1])
            s_new = decay * s_ref[j, h] + kt_ref[0, ks, j:j + 1] * v_ref[j:j + 1, vs]
            sout_ref[j, h] = s_new
            o = jnp.sum(qt_ref[0, ks, j:j + 1] * s_new, axis=0, keepdims=True)
            o = o * lax.rsqrt(jnp.mean(o * o, axis=-1, keepdims=True) + LN_EPS) * gam_ref[...]
            yb_ref[j:j + 1, vs] = (sr_ref[j:j + 1, vs] * o).astype(BF16)


def _gla_step(q, k, g, v, sr, gamma, state):
    t, dqk = q.shape
    dvt = v.shape[1]
    dk, dv = dqk // GLA_HEADS, dvt // GLA_HEADS
    steps = t // DEC_TOKENS
    tr = lambda a: a.reshape(steps, DEC_TOKENS, dqk).transpose(0, 2, 1)
    col = pl.BlockSpec((1, dqk, DEC_TOKENS), lambda i: (i, 0, 0))
    row = lambda n: pl.BlockSpec((DEC_TOKENS, n), lambda i: (i, 0))
    st = pl.BlockSpec((DEC_TOKENS, GLA_HEADS, dk, dv), lambda i: (i, 0, 0, 0))
    return pl.pallas_call(
        functools.partial(_gla_step_kernel, dk=dk, dv=dv),
        grid=(steps,),
        in_specs=[col, col, col, row(dvt), row(dvt), _const_spec(gamma.shape), st],
        out_specs=[row(dvt), st],
        out_shape=[jax.ShapeDtypeStruct((t, dvt), BF16), jax.ShapeDtypeStruct(state.shape, F32)],
        compiler_params=_params(("arbitrary",)),
        name="gla_step",
    )(tr(q), tr(k), tr(g), v, sr, gamma, state)


def _merge_kernel(ma_ref, sgb_ref, yb_ref, x_ref, wbrb_ref, wo_ref, g_ref, b_ref, x1_ref, x1t_ref, *, alpha):
    m = ma_ref[...] + sgb_ref[...] * _dot(yb_ref[...], wbrb_ref[...])
    x1 = _layernorm(alpha * x_ref[...] + _dot(m.astype(BF16), wo_ref[...]), g_ref[...], b_ref[...])
    x1_ref[...] = x1
    x1t_ref[...] = x1.T.astype(BF16)


def _merge(ma, sgb, yb, x, w, *, tm, alpha):
    t, d = x.shape
    row = pl.BlockSpec((tm, d), lambda i: (i, 0))
    weights = [w['wbrb'], w['wo'], w['ln1g'], w['ln1b']]
    return pl.pallas_call(
        functools.partial(_merge_kernel, alpha=alpha),
        grid=(t // tm,),
        in_specs=[row, row, row, row] + [_const_spec(a.shape) for a in weights],
        out_specs=[row, pl.BlockSpec((d, tm), lambda i: (0, i))],
        out_shape=[jax.ShapeDtypeStruct((t, d), F32), jax.ShapeDtypeStruct((d, t), BF16)],
        compiler_params=_params(("arbitrary",)),
        name="merge_ln1",
    )(ma, sgb, yb, x, *weights)


CAND_COUNTS = tuple(PEER_TOPK // (r1 + 1) for r1 in range(PEER_TOPK))
CAND_OFFSETS = tuple(sum(CAND_COUNTS[:r1]) for r1 in range(PEER_TOPK))
CAND_ROWS = -(-sum(CAND_COUNTS) // 8) * 8


SUBLANES = 8
KEY_TILES = PEER_NKEYS // SUBLANES
assert KEY_TILES == PEER_TOPK


def _odd_even_merge(lo, hi, r):
    step = r * 2
    if step < hi - lo:
        yield from _odd_even_merge(lo, hi, step)
        yield from _odd_even_merge(lo + r, hi, step)
        yield from [(i, i + r) for i in range(lo + r, hi - r, step)]
    else:
        yield (lo, lo + r)


def _odd_even_merge_sort(lo, hi):
    if hi - lo >= 1:
        mid = lo + (hi - lo) // 2
        yield from _odd_even_merge_sort(lo, mid)
        yield from _odd_even_merge_sort(mid + 1, hi)
        yield from _odd_even_merge(lo, hi, 1)


SORT_NET = tuple(_odd_even_merge_sort(0, KEY_TILES - 1))


def _compare_exchange(v, i, j):
    v[i], v[j] = jnp.maximum(v[i], v[j]), jnp.minimum(v[i], v[j])


def _sorted_top(s3):
    v = [s3[i] for i in range(KEY_TILES)]
    for i, j in SORT_NET:
        _compare_exchange(v, i, j)
    shift = SUBLANES // 2
    while shift >= 1:
        other = [pltpu.roll(x, shift, axis=0) for x in v]
        v = [jnp.maximum(v[i], other[PEER_TOPK - 1 - i]) for i in range(PEER_TOPK)]
        d = PEER_TOPK // 2
        while d >= 1:
            for i in range(PEER_TOPK):
                if i & d == 0:
                    _compare_exchange(v, i, i + d)
            d //= 2
        shift //= 2
    return v


def _select_top_exact(s, vals_ref):
    n = s.shape[0]
    iota = lax.broadcasted_iota(jnp.int32, s.shape, 0).astype(F32)
    rank = jnp.full(s.shape, NOT_SELECTED, F32)
    for r in range(PEER_TOPK):
        m = jnp.max(s, axis=0, keepdims=True)
        first = jnp.min(jnp.where(s == m, iota, float(n)), axis=0, keepdims=True)
        sel = iota == first
        rank = jnp.where(sel, float(r), rank)
        vals_ref[r:r + 1, :] = m
        s = jnp.where(sel, -jnp.inf, s)
    return rank


def _count(flags):
    return jnp.sum(jnp.where(flags, 1.0, 0.0), axis=0, keepdims=True)


def _top_values(s, vals_ref):
    tm = s.shape[1]
    kk = PEER_TOPK
    s3 = s.reshape(KEY_TILES, SUBLANES, tm)
    v = _sorted_top(s3)
    for r in range(kk):
        vals_ref[r:r + 1, :] = v[r][0:1, :]
    reach = jnp.sum(jnp.sum(jnp.where(s3 >= v[kk - 1][None], 1.0, 0.0), axis=0), axis=0, keepdims=True)
    equal = jnp.zeros((SUBLANES, tm), F32)
    for r in range(kk - 1):
        equal = equal + jnp.where(v[r] == v[r + 1], 1.0, 0.0)
    return v, jnp.abs(reach - kk) + equal[0:1, :]


def _choose_pairs(vals_ref, cand_ref, exact_ties):
    tm = vals_ref.shape[2]
    kk = PEER_TOPK
    cand_ref[...] = jnp.full(cand_ref.shape, -jnp.inf, F32)
    for r1 in range(kk):
        n2, off = CAND_COUNTS[r1], CAND_OFFSETS[r1]
        cand_ref[off:off + n2, :] = vals_ref[0, r1:r1 + 1, :] + vals_ref[1, 0:n2, :]
    start = cand_ref[...]
    cand = start
    iota_c = lax.broadcasted_iota(jnp.int32, cand.shape, 0).astype(F32) if exact_ties else None
    best = vals_ref[0, 0:1, :] + vals_ref[1, 0:1, :]
    z = jnp.zeros((1, tm), F32)
    for r in range(kk):
        m = jnp.max(cand, axis=0, keepdims=True)
        sel = cand == m
        if exact_ties:
            first = jnp.min(jnp.where(sel, iota_c, float(CAND_ROWS)), axis=0, keepdims=True)
            sel = iota_c == first
        z = z + jnp.exp(m - best)
        cand = jnp.where(sel, -jnp.inf, cand)
    chosen = jnp.where(start != cand, 1.0, 0.0)
    return chosen, z, jnp.abs(jnp.sum(chosen, axis=0, keepdims=True) - kk)


def _write_head(h, s1, s2, key1, match, rank2, chosen, z, vals_ref, rank2_ref, e2_ref, cnt_ref, c_ref):
    tm = s1.shape[1]
    key1 = key1.reshape(KEY_TILES, SUBLANES, tm)
    cnt = jnp.zeros(key1.shape, F32)
    for r1 in range(PEER_TOPK):
        n2, off = CAND_COUNTS[r1], CAND_OFFSETS[r1]
        count = jnp.sum(chosen[off:off + n2, :], axis=0, keepdims=True)
        cnt = cnt + jnp.where(key1 == match[r1][None], count[None], 0.0)
    rank2_ref[h] = rank2.astype(BF16)
    cnt_ref[h] = cnt.reshape(PEER_NKEYS, tm)
    c_ref[h] = jnp.exp(s1 - vals_ref[0, 0:1, :]) / z
    e2_ref[h] = jnp.exp(s2 - vals_ref[1, 0:1, :]).astype(BF16)


def _route_kernel(x1_ref, wq_ref, keys_ref, rank2_ref, e2_ref, cnt_ref, c_ref, s_ref, vals_ref, cand_ref, bad_ref):
    tm = x1_ref.shape[0]
    kk = PEER_TOPK
    outs = (rank2_ref, e2_ref, cnt_ref, c_ref)
    q = _dot(x1_ref[...].astype(BF16), wq_ref[...])
    for hp in range(2 * PEER_HEADS):
        qq = q[:, hp * LANES:(hp + 1) * LANES]
        qc = qq - jnp.mean(qq, axis=-1, keepdims=True)
        qn = (qc * lax.rsqrt(jnp.mean(qc * qc, axis=-1, keepdims=True) + LN_EPS)).astype(BF16)
        s_ref[hp] = _dot_nt(keys_ref[hp], qn)

    any_bad = jnp.zeros((1, tm), F32)
    for h in range(PEER_HEADS):
        s1, s2 = s_ref[2 * h], s_ref[2 * h + 1]
        v1, bad1 = _top_values(s1, vals_ref.at[h, 0])
        v2, bad2 = _top_values(s2, vals_ref.at[h, 1])
        s2t = s2.reshape(KEY_TILES, SUBLANES, tm)
        rank2 = jnp.zeros(s2t.shape, F32)
        for r in range(kk):
            rank2 = jnp.where(s2t < v2[r][None], float(r + 1), rank2)
        chosen, z, bad3 = _choose_pairs(vals_ref.at[h], cand_ref.at[h], False)
        _write_head(h, s1, s2, s1, v1, rank2.reshape(PEER_NKEYS, tm), chosen, z, vals_ref.at[h], *outs)
        bad = bad1 + bad2 + bad3
        bad_ref[h:h + 1, :] = bad
        any_bad = jnp.maximum(any_bad, bad)

    @pl.when(jnp.max(any_bad) > 0.0)
    def _():
        def redo(h, carry):
            @pl.when(jnp.max(bad_ref[pl.ds(h, 1), :]) > 0.0)
            def _():
                s1, s2 = s_ref[2 * h], s_ref[2 * h + 1]
                rank1 = _select_top_exact(s1, vals_ref.at[0, 0])
                rank2 = _select_top_exact(s2, vals_ref.at[0, 1])
                chosen, z, _ = _choose_pairs(vals_ref.at[0], cand_ref.at[0], True)
                ranks = [jnp.full((SUBLANES, tm), float(r), F32) for r in range(kk)]
                _write_head(h, s1, s2, rank1, ranks, rank2, chosen, z, vals_ref.at[0], *outs)
            return carry

        lax.fori_loop(0, PEER_HEADS, redo, 0)


def _route(x1, wq, keys, *, tm):
    t, d = x1.shape
    fac = pl.BlockSpec((PEER_HEADS, PEER_NKEYS, tm), lambda i: (0, 0, i))
    shape = lambda dt: jax.ShapeDtypeStruct((PEER_HEADS, PEER_NKEYS, t), dt)
    return pl.pallas_call(
        _route_kernel,
        grid=(t // tm,),
        in_specs=[pl.BlockSpec((tm, d), lambda i: (i, 0)), _const_spec(wq.shape), _const_spec(keys.shape)],
        out_specs=[fac, fac, fac, fac],
        out_shape=[shape(BF16), shape(BF16), shape(F32), shape(F32)],
        scratch_shapes=[pltpu.VMEM((2 * PEER_HEADS, PEER_NKEYS, tm), F32),
                        pltpu.VMEM((PEER_HEADS, 2, PEER_TOPK, tm), F32),
                        pltpu.VMEM((PEER_HEADS, CAND_ROWS, tm), F32),
                        pltpu.VMEM((PEER_HEADS, tm), F32)],
        compiler_params=_params(("arbitrary",)),
        name="peer_route",
    )(x1, wq, keys)


EXPERT_TILE = 1024
GROUP = 2
GATE_COLS = 256
BF16_ROWS = 16


def _experts_kernel(x1t_ref, u_ref, vt_ref, rank2_ref, e2_ref, cnt_ref, c_ref, o_ref, acc_ref, wt_ref, *, na):
    j = pl.program_id(1)
    tm = x1t_ref.shape[1]
    ch = min(GATE_COLS, tm)
    tiles = PEER_NKEYS // BF16_ROWS

    @pl.when(j == 0)
    def _():
        acc_ref[...] = jnp.zeros_like(acc_ref)

    rows = GROUP * PEER_NKEYS

    for g in range(na // GROUP):
        r0 = g * rows
        ht = _dot(u_ref[r0:r0 + rows, :], x1t_ref[...])
        for k in range(GROUP):
            row = slice(g * GROUP + k, g * GROUP + k + 1)
            for cc in range(tm // ch):
                cols = slice(cc * ch, (cc + 1) * ch)
                gate = None
                for h in range(PEER_HEADS):
                    cnt16 = jnp.broadcast_to(cnt_ref[h, row, cols], (BF16_ROWS, ch)).astype(BF16)
                    c16 = jnp.broadcast_to(c_ref[h, row, cols], (BF16_ROWS, ch)).astype(BF16)
                    r2 = rank2_ref[h, :, cols].reshape(tiles, BF16_ROWS, ch)
                    e2 = e2_ref[h, :, cols].reshape(tiles, BF16_ROWS, ch)
                    term = jnp.where(r2 < cnt16[None], e2 * c16[None], jnp.zeros((), BF16))
                    gate = term if gate is None else gate + term
                hk = _gelu(ht[k * PEER_NKEYS:(k + 1) * PEER_NKEYS, cols]).astype(BF16).reshape(tiles, BF16_ROWS, ch)
                wt_ref[r0 + k * PEER_NKEYS:r0 + (k + 1) * PEER_NKEYS, cols] = (hk * gate).reshape(PEER_NKEYS, ch)
    acc_ref[...] += _dot(vt_ref[0], wt_ref[...])

    @pl.when(j == pl.num_programs(1) - 1)
    def _():
        o_ref[...] = acc_ref[...].T


def _experts(x1t, u, vt, rank2, e2, cnt, c, *, tm, tn):
    d, t = x1t.shape
    n = u.shape[0]
    na = tn // PEER_NKEYS
    fac = pl.BlockSpec((PEER_HEADS, PEER_NKEYS, tm), lambda i, j: (0, 0, i))
    rows = pl.BlockSpec((PEER_HEADS, na, tm), lambda i, j: (0, j, i))
    return pl.pallas_call(
        functools.partial(_experts_kernel, na=na),
        grid=(t // tm, n // tn),
        in_specs=[pl.BlockSpec((d, tm), lambda i, j: (0, i)),
                  pl.BlockSpec((tn, d), lambda i, j: (j, 0)),
                  pl.BlockSpec((1, d, tn), lambda i, j: (j, 0, 0)),
                  fac, fac, rows, rows],
        out_specs=pl.BlockSpec((tm, d), lambda i, j: (i, 0)),
        out_shape=jax.ShapeDtypeStruct((t, d), F32),
        scratch_shapes=[pltpu.VMEM((d, tm), F32), pltpu.VMEM((tn, tm), BF16)],
        compiler_params=_params(("arbitrary", "arbitrary")),
        name="peer_experts",
    )(x1t, u, vt, rank2, e2, cnt, c)


def _final_kernel(x1_ref, peer_ref, p_ref, g_ref, b_ref, wpe_ref, wpg_ref, o_ref, *, alpha):
    x2 = _layernorm(alpha * x1_ref[...] + peer_ref[...], g_ref[...], b_ref[...])
    emb = _dot(p_ref[...].astype(BF16), wpe_ref[...])
    o_ref[...] = x2 + emb * jax.nn.sigmoid(_dot(x2.astype(BF16), wpg_ref[...]))


def _final(x1, peer, p, w, *, tm, alpha):
    t, d = x1.shape
    row = pl.BlockSpec((tm, d), lambda i: (i, 0))
    weights = [w['ln2g'], w['ln2b'], w['wpe'], w['wpg']]
    return pl.pallas_call(
        functools.partial(_final_kernel, alpha=alpha),
        grid=(t // tm,),
        in_specs=[row, row, pl.BlockSpec((tm, p.shape[1]), lambda i: (i, 0))]
                 + [_const_spec(a.shape) for a in weights],
        out_specs=row,
        out_shape=jax.ShapeDtypeStruct((t, d), F32),
        compiler_params=_params(("arbitrary",)),
        name="ln2_embed",
    )(x1, peer, p, *weights)


def _prep_weights(w_in, w_s, b_s, ln_v_g, ln_v_b, w_gk, b_gk, gla_norm_g, w_br_a, w_br_b, w_o, ln1_g, ln1_b,
                  peer_wq, peer_keys, peer_u, peer_v, ln2_g, ln2_b, w_pe, w_pg):
    d = w_in.shape[0]
    aw = ln_v_g.shape[0]
    dqk = w_gk.shape[1]
    dvt = w_br_b.shape[0]
    rank = w_gk.shape[0]
    sizes = (aw, aw, dqk, dqk, dvt, dvt, rank, d, d)
    offs = [0]
    for s in sizes:
        offs.append(offs[-1] + s)
    cols = lambda i: w_in[:, offs[i]:offs[i + 1]].astype(BF16)
    r2 = lambda a: a.reshape(1, -1)
    gdim = aw // A_GROUPS
    nk = peer_keys.shape[2]
    return {
        'wua': cols(0), 'wva': cols(1), 'wq': cols(2), 'wk': cols(3), 'wv': cols(4), 'wr': cols(5),
        'wgl': jnp.pad(cols(6), ((0, 0), (0, LANES - rank))),
        'wgk': jnp.pad(w_gk.astype(BF16), ((0, LANES - rank), (0, 0))),
        'bgk': r2(b_gk), 'wga': cols(7), 'wgb': cols(8),
        'ws': w_s,
        'bsx': jnp.repeat(b_s.T, gdim, axis=1),
        'ws00': r2(jnp.repeat(w_s[:, 0, 0], gdim)),
        'bs0': r2(jnp.repeat(b_s[:, 0], gdim)),
        'lng': r2(ln_v_g), 'lnb': r2(ln_v_b),
        'wbra': w_br_a.astype(BF16), 'wbrb': w_br_b.astype(BF16), 'wo': w_o.astype(BF16),
        'gamma': r2(gla_norm_g), 'ln1g': r2(ln1_g), 'ln1b': r2(ln1_b),
        'pwq': peer_wq.astype(BF16),
        'keys': peer_keys.astype(BF16).reshape(-1, nk, peer_keys.shape[3]),
        'pu': peer_u.astype(BF16),
        'pvt': peer_v.astype(BF16).reshape(-1, EXPERT_TILE, peer_v.shape[1]).transpose(0, 2, 1),
        'ln2g': r2(ln2_g), 'ln2b': r2(ln2_b), 'wpe': w_pe.astype(BF16), 'wpg': w_pg.astype(BF16),
    }


def _block(t, cap):
    best = LANES
    for m in range(LANES, cap + 1, LANES):
        if t % m == 0:
            best = m
    return best


def _layer(x, p, state, w, *, alpha):
    bsz, seq_len, d = x.shape
    t = bsz * seq_len
    x2d = x.reshape(t, d)
    decode = seq_len == 1
    assert decode or (seq_len % A_CHUNK == 0 and state is None)
    assert t % LANES == 0
    if decode:
        ma, q, k, v, b, sr, sgb, cv = _stage1(x2d, w, decode=True, tm=_block(t, 256), seq_len=1)
        yb, s_new = _gla_step(q, k, b, v, sr, w['gamma'], state)
        cv = cv.reshape(bsz, 1, d)
    else:
        ma, q, k, v, b, sr, sgb, cv = _stage1(x2d, w, decode=False, tm=_block(seq_len, 256), seq_len=seq_len)
        yb, s_new = _gla_prompt(q, k, b, v, sr, w['gamma'], bsz=bsz, seq_len=seq_len)
    x1, x1t = _merge(ma, sgb, yb, x2d, w, tm=_block(t, 512), alpha=alpha)
    rank2, e2, cnt, c = _route(x1, w['pwq'], w['keys'], tm=LANES)
    peer = _experts(x1t, w['pu'], w['pvt'], rank2, e2, cnt, c, tm=_block(t, 1024), tn=EXPERT_TILE)
    x3 = _final(x1, peer, p.reshape(t, -1), w, tm=_block(t, 512), alpha=alpha)
    return x3.reshape(bsz, seq_len, d), s_new, cv


def kernel(x_prompt, x_sample, state_gla, p_prompt, p_sample, w_in, w_s, b_s, ln_v_g, ln_v_b, w_gk, b_gk,
           gla_norm_g, w_br_a, w_br_b, w_o, ln1_g, ln1_b, peer_wq, peer_keys, peer_u, peer_v, ln2_g, ln2_b,
           w_pe, w_pg):
    depth = w_in.shape[0]
    alpha = (2.0 * depth) ** 0.25
    yp, ys = x_prompt, x_sample
    gla_p, gla_s, cv_p, cv_s = [], [], [], []
    for i in range(depth):
        w = _prep_weights(w_in[i], w_s[i], b_s[i], ln_v_g[i], ln_v_b[i], w_gk[i], b_gk[i], gla_norm_g[i],
                          w_br_a[i], w_br_b[i], w_o[i], ln1_g[i], ln1_b[i], peer_wq[i], peer_keys[i],
                          peer_u[i], peer_v[i], ln2_g[i], ln2_b[i], w_pe[i], w_pg[i])
        yp, sp, vp = _layer(yp, p_prompt[i], None, w, alpha=alpha)
        ys, ss, vs = _layer(ys, p_sample[i], state_gla[i], w, alpha=alpha)
        gla_p.append(sp)
        gla_s.append(ss)
        cv_p.append(vp)
        cv_s.append(vs)
    return (yp, ys, jnp.stack(gla_p), jnp.stack(gla_s), jnp.stack(cv_p), jnp.stack(cv_s))
```

```python
import functools
import math

import jax
import jax.numpy as jnp
from jax import lax
from jax.experimental import pallas as pl
from jax.experimental.pallas import tpu as pltpu

F32 = jnp.float32
BF16 = jnp.bfloat16

LANES = 128
A_CHUNK = 128
A_GROUPS = 8
GLA_HEADS = 4
GLA_TAU = 16.0
GLA_CHUNK = 128
PEER_HEADS = 8
PEER_NKEYS = 128
PEER_TOPK = 16
LN_EPS = 1e-5
NOT_SELECTED = 255.0
LOG2_E = math.log2(math.e)
VMEM_LIMIT = 56 * 1024 * 1024


def _dot(a, b):
    return jnp.dot(a, b, preferred_element_type=F32)


def _dot_nt(a, b):
    return lax.dot_general(a, b, (((1,), (1,)), ((), ())), preferred_element_type=F32)


def _layernorm(x, g, b):
    mu = jnp.mean(x, axis=-1, keepdims=True)
    xc = x - mu
    var = jnp.mean(xc * xc, axis=-1, keepdims=True)
    return xc * lax.rsqrt(var + LN_EPS) * g + b


def _gelu(x):
    k = -2.0 * math.sqrt(2.0 / math.pi) * math.log2(math.e)
    return x / (1.0 + jnp.exp2((x * x * (0.044715 * k) + k) * x))


def _log_sigmoid(z):
    return jnp.minimum(z, 0.0) - jnp.log1p(jnp.exp(-jnp.abs(z)))


def _const_spec(shape):
    nd = len(shape)
    return pl.BlockSpec(shape, lambda *_: (0,) * nd, pipeline_mode=pl.Buffered(1))


def _params(semantics):
    return pltpu.CompilerParams(dimension_semantics=semantics, vmem_limit_bytes=VMEM_LIMIT)


def _stage1_kernel(x_ref, wua_ref, wva_ref, wq_ref, wk_ref, wv_ref, wr_ref, wgl_ref, wgk_ref, bgk_ref,
                   wga_ref, wgb_ref, ws_ref, bs_ref, lng_ref, lnb_ref, wbra_ref,
                   ma_ref, q_ref, k_ref, v_ref, b_ref, sr_ref, sgb_ref, cv_ref, ya_ref, *, decode, tm, dk):
    xb = x_ref[...].astype(BF16)
    gu = jax.nn.gelu(_dot(xb, wua_ref[...]))
    van = _layernorm(jax.nn.gelu(_dot(xb, wva_ref[...])), lng_ref[...], lnb_ref[...])
    if decode:
        cv_ref[...] = van
        ya_ref[...] = (gu * (van * ws_ref[...] + bs_ref[...])).astype(BF16)
    else:
        cv_ref[0] = van[tm - A_CHUNK:, :]
        vb = van.astype(BF16)
        row = lax.broadcasted_iota(jnp.int32, (A_CHUNK, A_CHUNK), 0)
        col = lax.broadcasted_iota(jnp.int32, (A_CHUNK, A_CHUNK), 1)
        for g in range(A_GROUPS):
            w = jnp.where(row >= col, ws_ref[g], 0.0).astype(BF16)
            cs = slice(g * LANES, (g + 1) * LANES)
            for c in range(tm // A_CHUNK):
                rs = slice(c * A_CHUNK, (c + 1) * A_CHUNK)
                z = _dot(w, vb[rs, cs]) + bs_ref[:, cs]
                ya_ref[rs, cs] = (gu[rs, cs] * z).astype(BF16)
    ma_ref[...] = jax.nn.sigmoid(_dot(xb, wga_ref[...])) * _dot(ya_ref[...], wbra_ref[...])
    q_ref[...] = _dot(xb, wq_ref[...]) * (dk ** -0.5)
    k_ref[...] = _dot(xb, wk_ref[...])
    v_ref[...] = _dot(xb, wv_ref[...])
    sr_ref[...] = jax.nn.silu(_dot(xb, wr_ref[...]))
    sgb_ref[...] = jax.nn.sigmoid(_dot(xb, wgb_ref[...]))
    gk_low = _dot(xb, wgl_ref[...]).astype(BF16)
    log_a = _log_sigmoid(_dot(gk_low, wgk_ref[...]) + bgk_ref[...]) / GLA_TAU
    if not decode:
        pos = lax.broadcasted_iota(jnp.int32, log_a.shape, 0) % GLA_CHUNK
        shift = 1
        while shift < GLA_CHUNK:
            log_a = log_a + jnp.where(pos >= shift, pltpu.roll(log_a, shift, axis=0), 0.0)
            shift *= 2
    b_ref[...] = log_a


def _stage1(x, w, *, decode, tm, seq_len):
    t, d = x.shape
    dqk = w['wq'].shape[1]
    dv = w['wv'].shape[1]
    steps = t // tm
    row = lambda n: pl.BlockSpec((tm, n), lambda i: (i, 0))
    if decode:
        cv_shape = jax.ShapeDtypeStruct((t, d), F32)
        cv_spec = row(d)
        ws, bs = w['ws00'], w['bs0']
    else:
        per_seq = seq_len // tm
        cv_shape = jax.ShapeDtypeStruct((t // seq_len, A_CHUNK, d), F32)
        cv_spec = pl.BlockSpec((1, A_CHUNK, d), lambda i: (i // per_seq, 0, 0))
        ws, bs = w['ws'], w['bsx']
    weights = [w['wua'], w['wva'], w['wq'], w['wk'], w['wv'], w['wr'], w['wgl'], w['wgk'], w['bgk'],
               w['wga'], w['wgb'], ws, bs, w['lng'], w['lnb'], w['wbra']]
    out_shape = [jax.ShapeDtypeStruct((t, d), F32),
                 jax.ShapeDtypeStruct((t, dqk), F32),
                 jax.ShapeDtypeStruct((t, dqk), F32),
                 jax.ShapeDtypeStruct((t, dv), F32),
                 jax.ShapeDtypeStruct((t, dqk), F32),
                 jax.ShapeDtypeStruct((t, dv), F32),
                 jax.ShapeDtypeStruct((t, d), F32),
                 cv_shape]
    out_specs = [row(d), row(dqk), row(dqk), row(dv), row(dqk), row(dv), row(d), cv_spec]
    return pl.pallas_call(
        functools.partial(_stage1_kernel, decode=decode, tm=tm, dk=dqk // GLA_HEADS),
        grid=(steps,),
        in_specs=[row(d)] + [_const_spec(a.shape) for a in weights],
        out_specs=out_specs,
        out_shape=out_shape,
        scratch_shapes=[pltpu.VMEM((tm, d), BF16)],
        compiler_params=_params(("arbitrary",)),
        name="stage1_decode" if decode else "stage1_prompt",
    )(x, *weights)


def _block_ref_rows(b, w):
    c, n = b.shape
    if w >= 8:
        r = b.reshape(c // (2 * w), 2 * w, n)
        return jnp.broadcast_to(r[:, w - 1:w, :], r.shape).reshape(c, n)
    r = b.reshape(c // 8, 8, n)
    sub = lax.broadcasted_iota(jnp.int32, r.shape, 1)
    pick = lambda j: jnp.broadcast_to(r[:, j:j + 1, :], r.shape)
    if w == 4:
        out = pick(3)
    elif w == 2:
        out = jnp.where(sub < 4, pick(1), pick(5))
    else:
        out = jnp.where(sub < 2, pick(0), jnp.where(sub < 4, pick(2), jnp.where(sub < 6, pick(4), pick(6))))
    return out.reshape(c, n)


def _gla_kernel(q_ref, k_ref, b_ref, v_ref, sr_ref, gam_ref, yb_ref, sfin_ref, s_ref, *, dk, dv, chunks):
    c = pl.program_id(1)

    @pl.when(c == 0)
    def _():
        s_ref[...] = jnp.zeros_like(s_ref)

    cc = GLA_CHUNK
    row = lax.broadcasted_iota(jnp.int32, (cc, cc), 0)
    col = lax.broadcasted_iota(jnp.int32, (cc, cc), 1)
    rowk = lax.broadcasted_iota(jnp.int32, (cc, dk), 0)
    for h in range(GLA_HEADS):
        ks = slice(h * dk, (h + 1) * dk)
        vs = slice(h * dv, (h + 1) * dv)
        s_old = s_ref[h]
        for ci in range(chunks):
            rs = slice(ci * cc, (ci + 1) * cc)
            qh, kh = q_ref[rs, ks], k_ref[rs, ks]
            bh = b_ref[rs, ks] * LOG2_E
            vb = v_ref[rs, vs].astype(BF16)
            o = _dot((qh * jnp.exp2(bh)).astype(BF16), s_old.astype(BF16))
            att = jnp.where(row == col, jnp.sum(qh * kh, axis=1, keepdims=True), 0.0)
            w = cc // 2
            while w >= 1:
                diff = bh - _block_ref_rows(bh, w)
                factor = jnp.exp2(jnp.minimum(diff, -diff))
                second = (rowk // w) % 2 == 1
                ql = jnp.where(second, qh * factor, 0.0)
                kl = jnp.where(second, 0.0, kh * factor)
                a_l = _dot_nt(ql.astype(BF16), kl.astype(BF16))
                att = att + jnp.where(row // (2 * w) == col // (2 * w), a_l, 0.0)
                w //= 2
            o = o + _dot(att.astype(BF16), vb)
            b_last = bh[cc - 1:cc, :]
            kdec_t = (kh * jnp.exp2(b_last - bh)).T.astype(BF16)
            decay_col = jnp.exp2(bh.T[:, cc - 1:cc])
            s_old = decay_col * s_old + _dot(kdec_t, vb)
            o = o * lax.rsqrt(jnp.mean(o * o, axis=-1, keepdims=True) + LN_EPS) * gam_ref[...]
            yb_ref[rs, vs] = (sr_ref[rs, vs] * o).astype(BF16)
        s_ref[h] = s_old

    @pl.when(c == pl.num_programs(1) - 1)
    def _():
        sfin_ref[0] = s_ref[...]


GLA_STEP_CHUNKS = 2


def _gla_prompt(q, k, b, v, sr, gamma, *, bsz, seq_len):
    t, dqk = q.shape
    dvt = v.shape[1]
    dk, dv = dqk // GLA_HEADS, dvt // GLA_HEADS
    chunks = GLA_STEP_CHUNKS if seq_len % (GLA_STEP_CHUNKS * GLA_CHUNK) == 0 else 1
    rows = chunks * GLA_CHUNK
    nc = seq_len // rows
    row = lambda n: pl.BlockSpec((rows, n), lambda i, c: (i * nc + c, 0))
    return pl.pallas_call(
        functools.partial(_gla_kernel, dk=dk, dv=dv, chunks=chunks),
        grid=(bsz, nc),
        in_specs=[row(dqk), row(dqk), row(dqk), row(dvt), row(dvt), _const_spec(gamma.shape)],
        out_specs=[row(dvt), pl.BlockSpec((1, GLA_HEADS, dk, dv), lambda i, c: (i, 0, 0, 0))],
        out_shape=[jax.ShapeDtypeStruct((t, dvt), BF16),
                   jax.ShapeDtypeStruct((bsz, GLA_HEADS, dk, dv), F32)],
        scratch_shapes=[pltpu.VMEM((GLA_HEADS, dk, dv), F32)],
        compiler_params=_params(("arbitrary", "arbitrary")),
        name="gla_prompt",
    )(q, k, b, v, sr, gamma)


DEC_TOKENS = 16


def _gla_step_kernel(qt_ref, kt_ref, gt_ref, v_ref, sr_ref, gam_ref, s_ref, yb_ref, sout_ref, *, dk, dv):
    for j in range(DEC_TOKENS):
        for h in range(GLA_HEADS):
            ks = slice(h * dk, (h + 1) * dk)
            vs = slice(h * dv, (h + 1) * dv)
            decay = jnp.exp(gt_ref[0, ks, j:j + 1])
            s_new = decay * s_ref[j, h] + kt_ref[0, ks, j:j + 1] * v_ref[j:j + 1, vs]
            sout_ref[j, h] = s_new
            o = jnp.sum(qt_ref[0, ks, j:j + 1] * s_new, axis=0, keepdims=True)
            o = o * lax.rsqrt(jnp.mean(o * o, axis=-1, keepdims=True) + LN_EPS) * gam_ref[...]
            yb_ref[j:j + 1, vs] = (sr_ref[j:j + 1, vs] * o).astype(BF16)


def _gla_step(q, k, g, v, sr, gamma, state):
    t, dqk = q.shape
    dvt = v.shape[1]
    dk, dv = dqk // GLA_HEADS, dvt // GLA_HEADS
    steps = t // DEC_TOKENS
    tr = lambda a: a.reshape(steps, DEC_TOKENS, dqk).transpose(0, 2, 1)
    col = pl.BlockSpec((1, dqk, DEC_TOKENS), lambda i: (i, 0, 0))
    row = lambda n: pl.BlockSpec((DEC_TOKENS, n), lambda i: (i, 0))
    st = pl.BlockSpec((DEC_TOKENS, GLA_HEADS, dk, dv), lambda i: (i, 0, 0, 0))
    return pl.pallas_call(
        functools.partial(_gla_step_kernel, dk=dk, dv=dv),
        grid=(steps,),
        in_specs=[col, col, col, row(dvt), row(dvt), _const_spec(gamma.shape), st],
        out_specs=[row(dvt), st],
        out_shape=[jax.ShapeDtypeStruct((t, dvt), BF16), jax.ShapeDtypeStruct(state.shape, F32)],
        compiler_params=_params(("arbitrary",)),
        name="gla_step",
    )(tr(q), tr(k), tr(g), v, sr, gamma, state)


def _merge_kernel(ma_ref, sgb_ref, yb_ref, x_ref, wbrb_ref, wo_ref, g_ref, b_ref, x1_ref, x1t_ref, *, alpha):
    m = ma_ref[...] + sgb_ref[...] * _dot(yb_ref[...], wbrb_ref[...])
    x1 = _layernorm(alpha * x_ref[...] + _dot(m.astype(BF16), wo_ref[...]), g_ref[...], b_ref[...])
    x1_ref[...] = x1
    x1t_ref[...] = x1.T.astype(BF16)


def _merge(ma, sgb, yb, x, w, *, tm, alpha):
    t, d = x.shape
    row = pl.BlockSpec((tm, d), lambda i: (i, 0))
    weights = [w['wbrb'], w['wo'], w['ln1g'], w['ln1b']]
    return pl.pallas_call(
        functools.partial(_merge_kernel, alpha=alpha),
        grid=(t // tm,),
        in_specs=[row, row, row, row] + [_const_spec(a.shape) for a in weights],
        out_specs=[row, pl.BlockSpec((d, tm), lambda i: (0, i))],
        out_shape=[jax.ShapeDtypeStruct((t, d), F32), jax.ShapeDtypeStruct((d, t), BF16)],
        compiler_params=_params(("arbitrary",)),
        name="merge_ln1",
    )(ma, sgb, yb, x, *weights)


CAND_COUNTS = tuple(PEER_TOPK // (r1 + 1) for r1 in range(PEER_TOPK))
CAND_OFFSETS = tuple(sum(CAND_COUNTS[:r1]) for r1 in range(PEER_TOPK))
CAND_ROWS = -(-sum(CAND_COUNTS) // 8) * 8


SUBLANES = 8
KEY_TILES = PEER_NKEYS // SUBLANES
assert KEY_TILES == PEER_TOPK


def _odd_even_merge(lo, hi, r):
    step = r * 2
    if step < hi - lo:
        yield from _odd_even_merge(lo, hi, step)
        yield from _odd_even_merge(lo + r, hi, step)
        yield from [(i, i + r) for i in range(lo + r, hi - r, step)]
    else:
        yield (lo, lo + r)


def _odd_even_merge_sort(lo, hi):
    if hi - lo >= 1:
        mid = lo + (hi - lo) // 2
        yield from _odd_even_merge_sort(lo, mid)
        yield from _odd_even_merge_sort(mid + 1, hi)
        yield from _odd_even_merge(lo, hi, 1)


SORT_NET = tuple(_odd_even_merge_sort(0, KEY_TILES - 1))


def _compare_exchange(v, i, j):
    v[i], v[j] = jnp.maximum(v[i], v[j]), jnp.minimum(v[i], v[j])


def _sorted_top(s3):
    v = [s3[i] for i in range(KEY_TILES)]
    for i, j in SORT_NET:
        _compare_exchange(v, i, j)
    shift = SUBLANES // 2
    while shift >= 1:
        other = [pltpu.roll(x, shift, axis=0) for x in v]
        v = [jnp.maximum(v[i], other[PEER_TOPK - 1 - i]) for i in range(PEER_TOPK)]
        d = PEER_TOPK // 2
        while d >= 1:
            for i in range(PEER_TOPK):
                if i & d == 0:
                    _compare_exchange(v, i, i + d)
            d //= 2
        shift //= 2
    return v


def _select_top_exact(s, vals_ref):
    n = s.shape[0]
    iota = lax.broadcasted_iota(jnp.int32, s.shape, 0).astype(F32)
    rank = jnp.full(s.shape, NOT_SELECTED, F32)
    for r in range(PEER_TOPK):
        m = jnp.max(s, axis=0, keepdims=True)
        first = jnp.min(jnp.where(s == m, iota, float(n)), axis=0, keepdims=True)
        sel = iota == first
        rank = jnp.where(sel, float(r), rank)
        vals_ref[r:r + 1, :] = m
        s = jnp.where(sel, -jnp.inf, s)
    return rank


def _count(flags):
    return jnp.sum(jnp.where(flags, 1.0, 0.0), axis=0, keepdims=True)


def _top_values(s, vals_ref):
    tm = s.shape[1]
    kk = PEER_TOPK
    s3 = s.reshape(KEY_TILES, SUBLANES, tm)
    v = _sorted_top(s3)
    for r in range(kk):
        vals_ref[r:r + 1, :] = v[r][0:1, :]
    reach = jnp.sum(jnp.sum(jnp.where(s3 >= v[kk - 1][None], 1.0, 0.0), axis=0), axis=0, keepdims=True)
    equal = jnp.zeros((SUBLANES, tm), F32)
    for r in range(kk - 1):
        equal = equal + jnp.where(v[r] == v[r + 1], 1.0, 0.0)
    return v, jnp.abs(reach - kk) + equal[0:1, :]


def _choose_pairs(vals_ref, cand_ref, exact_ties):
    tm = vals_ref.shape[2]
    kk = PEER_TOPK
    cand_ref[...] = jnp.full(cand_ref.shape, -jnp.inf, F32)
    for r1 in range(kk):
        n2, off = CAND_COUNTS[r1], CAND_OFFSETS[r1]
        cand_ref[off:off + n2, :] = vals_ref[0, r1:r1 + 1, :] + vals_ref[1, 0:n2, :]
    start = cand_ref[...]
    cand = start
    iota_c = lax.broadcasted_iota(jnp.int32, cand.shape, 0).astype(F32) if exact_ties else None
    best = vals_ref[0, 0:1, :] + vals_ref[1, 0:1, :]
    z = jnp.zeros((1, tm), F32)
    for r in range(kk):
        m = jnp.max(cand, axis=0, keepdims=True)
        sel = cand == m
        if exact_ties:
            first = jnp.min(jnp.where(sel, iota_c, float(CAND_ROWS)), axis=0, keepdims=True)
            sel = iota_c == first
        z = z + jnp.exp(m - best)
        cand = jnp.where(sel, -jnp.inf, cand)
    chosen = jnp.where(start != cand, 1.0, 0.0)
    return chosen, z, jnp.abs(jnp.sum(chosen, axis=0, keepdims=True) - kk)


def _write_head(h, s1, s2, key1, match, rank2, chosen, z, vals_ref, rank2_ref, e2_ref, cnt_ref, c_ref):
    tm = s1.shape[1]
    key1 = key1.reshape(KEY_TILES, SUBLANES, tm)
    cnt = jnp.zeros(key1.shape, F32)
    for r1 in range(PEER_TOPK):
        n2, off = CAND_COUNTS[r1], CAND_OFFSETS[r1]
        count = jnp.sum(chosen[off:off + n2, :], axis=0, keepdims=True)
        cnt = cnt + jnp.where(key1 == match[r1][None], count[None], 0.0)
    rank2_ref[h] = rank2.astype(BF16)
    cnt_ref[h] = cnt.reshape(PEER_NKEYS, tm)
    c_ref[h] = jnp.exp(s1 - vals_ref[0, 0:1, :]) / z
    e2_ref[h] = jnp.exp(s2 - vals_ref[1, 0:1, :]).astype(BF16)


def _route_kernel(x1_ref, wq_ref, keys_ref, rank2_ref, e2_ref, cnt_ref, c_ref, s_ref, vals_ref, cand_ref, bad_ref):
    tm = x1_ref.shape[0]
    kk = PEER_TOPK
    outs = (rank2_ref, e2_ref, cnt_ref, c_ref)
    q = _dot(x1_ref[...].astype(BF16), wq_ref[...])
    for hp in range(2 * PEER_HEADS):
        qq = q[:, hp * LANES:(hp + 1) * LANES]
        qc = qq - jnp.mean(qq, axis=-1, keepdims=True)
        qn = (qc * lax.rsqrt(jnp.mean(qc * qc, axis=-1, keepdims=True) + LN_EPS)).astype(BF16)
        s_ref[hp] = _dot_nt(keys_ref[hp], qn)

    any_bad = jnp.zeros((1, tm), F32)
    for h in range(PEER_HEADS):
        s1, s2 = s_ref[2 * h], s_ref[2 * h + 1]
        v1, bad1 = _top_values(s1, vals_ref.at[h, 0])
        v2, bad2 = _top_values(s2, vals_ref.at[h, 1])
        s2t = s2.reshape(KEY_TILES, SUBLANES, tm)
        rank2 = jnp.zeros(s2t.shape, F32)
        for r in range(kk):
            rank2 = jnp.where(s2t < v2[r][None], float(r + 1), rank2)
        chosen, z, bad3 = _choose_pairs(vals_ref.at[h], cand_ref.at[h], False)
        _write_head(h, s1, s2, s1, v1, rank2.reshape(PEER_NKEYS, tm), chosen, z, vals_ref.at[h], *outs)
        bad = bad1 + bad2 + bad3
        bad_ref[h:h + 1, :] = bad
        any_bad = jnp.maximum(any_bad, bad)

    @pl.when(jnp.max(any_bad) > 0.0)
    def _():
        def redo(h, carry):
            @pl.when(jnp.max(bad_ref[pl.ds(h, 1), :]) > 0.0)
            def _():
                s1, s2 = s_ref[2 * h], s_ref[2 * h + 1]
                rank1 = _select_top_exact(s1, vals_ref.at[0, 0])
                rank2 = _select_top_exact(s2, vals_ref.at[0, 1])
                chosen, z, _ = _choose_pairs(vals_ref.at[0], cand_ref.at[0], True)
                ranks = [jnp.full((SUBLANES, tm), float(r), F32) for r in range(kk)]
                _write_head(h, s1, s2, rank1, ranks, rank2, chosen, z, vals_ref.at[0], *outs)
            return carry

        lax.fori_loop(0, PEER_HEADS, redo, 0)


def _route(x1, wq, keys, *, tm):
    t, d = x1.shape
    fac = pl.BlockSpec((PEER_HEADS, PEER_NKEYS, tm), lambda i: (0, 0, i))
    shape = lambda dt: jax.ShapeDtypeStruct((PEER_HEADS, PEER_NKEYS, t), dt)
    return pl.pallas_call(
        _route_kernel,
        grid=(t // tm,),
        in_specs=[pl.BlockSpec((tm, d), lambda i: (i, 0)), _const_spec(wq.shape), _const_spec(keys.shape)],
        out_specs=[fac, fac, fac, fac],
        out_shape=[shape(BF16), shape(BF16), shape(F32), shape(F32)],
        scratch_shapes=[pltpu.VMEM((2 * PEER_HEADS, PEER_NKEYS, tm), F32),
                        pltpu.VMEM((PEER_HEADS, 2, PEER_TOPK, tm), F32),
                        pltpu.VMEM((PEER_HEADS, CAND_ROWS, tm), F32),
                        pltpu.VMEM((PEER_HEADS, tm), F32)],
        compiler_params=_params(("arbitrary",)),
        name="peer_route",
    )(x1, wq, keys)


EXPERT_TILE = 1024
GROUP = 2
GATE_COLS = 256
BF16_ROWS = 16


def _experts_kernel(x1t_ref, u_ref, vt_ref, rank2_ref, e2_ref, cnt_ref, c_ref, o_ref, acc_ref, wt_ref, *, na):
    j = pl.program_id(1)
    tm = x1t_ref.shape[1]
    ch = min(GATE_COLS, tm)
    tiles = PEER_NKEYS // BF16_ROWS

    @pl.when(j == 0)
    def _():
        acc_ref[...] = jnp.zeros_like(acc_ref)

    rows = GROUP * PEER_NKEYS

    for g in range(na // GROUP):
        r0 = g * rows
        ht = _dot(u_ref[r0:r0 + rows, :], x1t_ref[...])
        for k in range(GROUP):
            row = slice(g * GROUP + k, g * GROUP + k + 1)
            for cc in range(tm // ch):
                cols = slice(cc * ch, (cc + 1) * ch)
                gate = None
                for h in range(PEER_HEADS):
                    cnt16 = jnp.broadcast_to(cnt_ref[h, row, cols], (BF16_ROWS, ch)).astype(BF16)
                    c16 = jnp.broadcast_to(c_ref[h, row, cols], (BF16_ROWS, ch)).astype(BF16)
                    r2 = rank2_ref[h, :, cols].reshape(tiles, BF16_ROWS, ch)
                    e2 = e2_ref[h, :, cols].reshape(tiles, BF16_ROWS, ch)
                    term = jnp.where(r2 < cnt16[None], e2 * c16[None], jnp.zeros((), BF16))
                    gate = term if gate is None else gate + term
                hk = _gelu(ht[k * PEER_NKEYS:(k + 1) * PEER_NKEYS, cols]).astype(BF16).reshape(tiles, BF16_ROWS, ch)
                wt_ref[r0 + k * PEER_NKEYS:r0 + (k + 1) * PEER_NKEYS, cols] = (hk * gate).reshape(PEER_NKEYS, ch)
    acc_ref[...] += _dot(vt_ref[0], wt_ref[...])

    @pl.when(j == pl.num_programs(1) - 1)
    def _():
        o_ref[...] = acc_ref[...].T


def _experts(x1t, u, vt, rank2, e2, cnt, c, *, tm, tn):
    d, t = x1t.shape
    n = u.shape[0]
    na = tn // PEER_NKEYS
    fac = pl.BlockSpec((PEER_HEADS, PEER_NKEYS, tm), lambda i, j: (0, 0, i))
    rows = pl.BlockSpec((PEER_HEADS, na, tm), lambda i, j: (0, j, i))
    return pl.pallas_call(
        functools.partial(_experts_kernel, na=na),
        grid=(t // tm, n // tn),
        in_specs=[pl.BlockSpec((d, tm), lambda i, j: (0, i)),
                  pl.BlockSpec((tn, d), lambda i, j: (j, 0)),
                  pl.BlockSpec((1, d, tn), lambda i, j: (j, 0, 0)),
                  fac, fac, rows, rows],
        out_specs=pl.BlockSpec((tm, d), lambda i, j: (i, 0)),
        out_shape=jax.ShapeDtypeStruct((t, d), F32),
        scratch_shapes=[pltpu.VMEM((d, tm), F32), pltpu.VMEM((tn, tm), BF16)],
        compiler_params=_params(("arbitrary", "arbitrary")),
        name="peer_experts",
    )(x1t, u, vt, rank2, e2, cnt, c)


def _final_kernel(x1_ref, peer_ref, p_ref, g_ref, b_ref, wpe_ref, wpg_ref, o_ref, *, alpha):
    x2 = _layernorm(alpha * x1_ref[...] + peer_ref[...], g_ref[...], b_ref[...])
    emb = _dot(p_ref[...].astype(BF16), wpe_ref[...])
    o_ref[...] = x2 + emb * jax.nn.sigmoid(_dot(x2.astype(BF16), wpg_ref[...]))


def _final(x1, peer, p, w, *, tm, alpha):
    t, d = x1.shape
    row = pl.BlockSpec((tm, d), lambda i: (i, 0))
    weights = [w['ln2g'], w['ln2b'], w['wpe'], w['wpg']]
    return pl.pallas_call(
        functools.partial(_final_kernel, alpha=alpha),
        grid=(t // tm,),
        in_specs=[row, row, pl.BlockSpec((tm, p.shape[1]), lambda i: (i, 0))]
                 + [_const_spec(a.shape) for a in weights],
        out_specs=row,
        out_shape=jax.ShapeDtypeStruct((t, d), F32),
        compiler_params=_params(("arbitrary",)),
        name="ln2_embed",
    )(x1, peer, p, *weights)


def _prep_weights(w_in, w_s, b_s, ln_v_g, ln_v_b, w_gk, b_gk, gla_norm_g, w_br_a, w_br_b, w_o, ln1_g, ln1_b,
                  peer_wq, peer_keys, peer_u, peer_v, ln2_g, ln2_b, w_pe, w_pg):
    d = w_in.shape[0]
    aw = ln_v_g.shape[0]
    dqk = w_gk.shape[1]
    dvt = w_br_b.shape[0]
    rank = w_gk.shape[0]
    sizes = (aw, aw, dqk, dqk, dvt, dvt, rank, d, d)
    offs = [0]
    for s in sizes:
        offs.append(offs[-1] + s)
    cols = lambda i: w_in[:, offs[i]:offs[i + 1]].astype(BF16)
    r2 = lambda a: a.reshape(1, -1)
    gdim = aw // A_GROUPS
    nk = peer_keys.shape[2]
    return {
        'wua': cols(0), 'wva': cols(1), 'wq': cols(2), 'wk': cols(3), 'wv': cols(4), 'wr': cols(5),
        'wgl': jnp.pad(cols(6), ((0, 0), (0, LANES - rank))),
        'wgk': jnp.pad(w_gk.astype(BF16), ((0, LANES - rank), (0, 0))),
        'bgk': r2(b_gk), 'wga': cols(7), 'wgb': cols(8),
        'ws': w_s,
        'bsx': jnp.repeat(b_s.T, gdim, axis=1),
        'ws00': r2(jnp.repeat(w_s[:, 0, 0], gdim)),
        'bs0': r2(jnp.repeat(b_s[:, 0], gdim)),
        'lng': r2(ln_v_g), 'lnb': r2(ln_v_b),
        'wbra': w_br_a.astype(BF16), 'wbrb': w_br_b.astype(BF16), 'wo': w_o.astype(BF16),
        'gamma': r2(gla_norm_g), 'ln1g': r2(ln1_g), 'ln1b': r2(ln1_b),
        'pwq': peer_wq.astype(BF16),
        'keys': peer_keys.astype(BF16).reshape(-1, nk, peer_keys.shape[3]),
        'pu': peer_u.astype(BF16),
        'pvt': peer_v.astype(BF16).reshape(-1, EXPERT_TILE, peer_v.shape[1]).transpose(0, 2, 1),
        'ln2g': r2(ln2_g), 'ln2b': r2(ln2_b), 'wpe': w_pe.astype(BF16), 'wpg': w_pg.astype(BF16),
    }


def _block(t, cap):
    best = LANES
    for m in range(LANES, cap + 1, LANES):
        if t % m == 0:
            best = m
    return best


def _layer(x, p, state, w, *, alpha):
    bsz, seq_len, d = x.shape
    t = bsz * seq_len
    x2d = x.reshape(t, d)
    decode = seq_len == 1
    assert decode or (seq_len % A_CHUNK == 0 and state is None)
    assert t % LANES == 0
    if decode:
        ma, q, k, v, b, sr, sgb, cv = _stage1(x2d, w, decode=True, tm=_block(t, 256), seq_len=1)
        yb, s_new = _gla_step(q, k, b, v, sr, w['gamma'], state)
        cv = cv.reshape(bsz, 1, d)
    else:
        ma, q, k, v, b, sr, sgb, cv = _stage1(x2d, w, decode=False, tm=_block(seq_len, 256), seq_len=seq_len)
        yb, s_new = _gla_prompt(q, k, b, v, sr, w['gamma'], bsz=bsz, seq_len=seq_len)
    x1, x1t = _merge(ma, sgb, yb, x2d, w, tm=_block(t, 512), alpha=alpha)
    rank2, e2, cnt, c = _route(x1, w['pwq'], w['keys'], tm=LANES)
    peer = _experts(x1t, w['pu'], w['pvt'], rank2, e2, cnt, c, tm=_block(t, 1024), tn=EXPERT_TILE)
    x3 = _final(x1, peer, p.reshape(t, -1), w, tm=_block(t, 512), alpha=alpha)
    return x3.reshape(bsz, seq_len, d), s_new, cv


def kernel(x_prompt, x_sample, state_gla, p_prompt, p_sample, w_in, w_s, b_s, ln_v_g, ln_v_b, w_gk, b_gk,
           gla_norm_g, w_br_a, w_br_b, w_o, ln1_g, ln1_b, peer_wq, peer_keys, peer_u, peer_v, ln2_g, ln2_b,
           w_pe, w_pg):
    depth = w_in.shape[0]
    alpha = (2.0 * depth) ** 0.25
    yp, ys = x_prompt, x_sample
    gla_p, gla_s, cv_p, cv_s = [], [], [], []
    for i in range(depth):
        w = _prep_weights(w_in[i], w_s[i], b_s[i], ln_v_g[i], ln_v_b[i], w_gk[i], b_gk[i], gla_norm_g[i],
                          w_br_a[i], w_br_b[i], w_o[i], ln1_g[i], ln1_b[i], peer_wq[i], peer_keys[i],
                          peer_u[i], peer_v[i], ln2_g[i], ln2_b[i], w_pe[i], w_pg[i])
        yp, sp, vp = _layer(yp, p_prompt[i], None, w, alpha=alpha)
        ys, ss, vs = _layer(ys, p_sample[i], state_gla[i], w, alpha=alpha)
        gla_p.append(sp)
        gla_s.append(ss)
        cv_p.append(vp)
        cv_s.append(vs)
    return (yp, ys, jnp.stack(gla_p), jnp.stack(gla_s), jnp.stack(cv_p), jnp.stack(cv_s))
```

```python
import functools
import math

import jax
import jax.numpy as jnp
from jax import lax
from jax.experimental import pallas as pl
from jax.experimental.pallas import tpu as pltpu

F32 = jnp.float32
BF16 = jnp.bfloat16

LANES = 128
A_CHUNK = 128
A_GROUPS = 8
GLA_HEADS = 4
GLA_TAU = 16.0
GLA_CHUNK = 128
PEER_HEADS = 8
PEER_NKEYS = 128
PEER_TOPK = 16
LN_EPS = 1e-5
NOT_SELECTED = 255.0
LOG2_E = math.log2(math.e)
VMEM_LIMIT = 56 * 1024 * 1024


def _dot(a, b):
    return jnp.dot(a, b, preferred_element_type=F32)


def _dot_nt(a, b):
    return lax.dot_general(a, b, (((1,), (1,)), ((), ())), preferred_element_type=F32)


def _layernorm(x, g, b):
    mu = jnp.mean(x, axis=-1, keepdims=True)
    xc = x - mu
    var = jnp.mean(xc * xc, axis=-1, keepdims=True)
    return xc * lax.rsqrt(var + LN_EPS) * g + b


def _gelu(x):
    k = -2.0 * math.sqrt(2.0 / math.pi) * math.log2(math.e)
    return x / (1.0 + jnp.exp2((x * x * (0.044715 * k) + k) * x))


def _log_sigmoid(z):
    return jnp.minimum(z, 0.0) - jnp.log1p(jnp.exp(-jnp.abs(z)))


def _const_spec(shape):
    nd = len(shape)
    return pl.BlockSpec(shape, lambda *_: (0,) * nd, pipeline_mode=pl.Buffered(1))


def _params(semantics):
    return pltpu.CompilerParams(dimension_semantics=semantics, vmem_limit_bytes=VMEM_LIMIT)


def _stage1_kernel(x_ref, wua_ref, wva_ref, wq_ref, wk_ref, wv_ref, wr_ref, wgl_ref, wgk_ref, bgk_ref,
                   wga_ref, wgb_ref, ws_ref, bs_ref, lng_ref, lnb_ref, wbra_ref,
                   ma_ref, q_ref, k_ref, v_ref, b_ref, sr_ref, sgb_ref, cv_ref, ya_ref, *, decode, tm, dk):
    xb = x_ref[...].astype(BF16)
    gu = jax.nn.gelu(_dot(xb, wua_ref[...]))
    van = _layernorm(jax.nn.gelu(_dot(xb, wva_ref[...])), lng_ref[...], lnb_ref[...])
    if decode:
        cv_ref[...] = van
        ya_ref[...] = (gu * (van * ws_ref[...] + bs_ref[...])).astype(BF16)
    else:
        cv_ref[0] = van[tm - A_CHUNK:, :]
        vb = van.astype(BF16)
        row = lax.broadcasted_iota(jnp.int32, (A_CHUNK, A_CHUNK), 0)
        col = lax.broadcasted_iota(jnp.int32, (A_CHUNK, A_CHUNK), 1)
        for g in range(A_GROUPS):
            w = jnp.where(row >= col, ws_ref[g], 0.0).astype(BF16)
            cs = slice(g * LANES, (g + 1) * LANES)
            for c in range(tm // A_CHUNK):
                rs = slice(c * A_CHUNK, (c + 1) * A_CHUNK)
                z = _dot(w, vb[rs, cs]) + bs_ref[:, cs]
                ya_ref[rs, cs] = (gu[rs, cs] * z).astype(BF16)
    ma_ref[...] = jax.nn.sigmoid(_dot(xb, wga_ref[...])) * _dot(ya_ref[...], wbra_ref[...])
    q_ref[...] = _dot(xb, wq_ref[...]) * (dk ** -0.5)
    k_ref[...] = _dot(xb, wk_ref[...])
    v_ref[...] = _dot(xb, wv_ref[...])
    sr_ref[...] = jax.nn.silu(_dot(xb, wr_ref[...]))
    sgb_ref[...] = jax.nn.sigmoid(_dot(xb, wgb_ref[...]))
    gk_low = _dot(xb, wgl_ref[...]).astype(BF16)
    log_a = _log_sigmoid(_dot(gk_low, wgk_ref[...]) + bgk_ref[...]) / GLA_TAU
    if not decode:
        pos = lax.broadcasted_iota(jnp.int32, log_a.shape, 0) % GLA_CHUNK
        shift = 1
        while shift < GLA_CHUNK:
            log_a = log_a + jnp.where(pos >= shift, pltpu.roll(log_a, shift, axis=0), 0.0)
            shift *= 2
    b_ref[...] = log_a


def _stage1(x, w, *, decode, tm, seq_len):
    t, d = x.shape
    dqk = w['wq'].shape[1]
    dv = w['wv'].shape[1]
    steps = t // tm
    row = lambda n: pl.BlockSpec((tm, n), lambda i: (i, 0))
    if decode:
        cv_shape = jax.ShapeDtypeStruct((t, d), F32)
        cv_spec = row(d)
        ws, bs = w['ws00'], w['bs0']
    else:
        per_seq = seq_len // tm
        cv_shape = jax.ShapeDtypeStruct((t // seq_len, A_CHUNK, d), F32)
        cv_spec = pl.BlockSpec((1, A_CHUNK, d), lambda i: (i // per_seq, 0, 0))
        ws, bs = w['ws'], w['bsx']
    weights = [w['wua'], w['wva'], w['wq'], w['wk'], w['wv'], w['wr'], w['wgl'], w['wgk'], w['bgk'],
               w['wga'], w['wgb'], ws, bs, w['lng'], w['lnb'], w['wbra']]
    out_shape = [jax.ShapeDtypeStruct((t, d), F32),
                 jax.ShapeDtypeStruct((t, dqk), F32),
                 jax.ShapeDtypeStruct((t, dqk), F32),
                 jax.ShapeDtypeStruct((t, dv), F32),
                 jax.ShapeDtypeStruct((t, dqk), F32),
                 jax.ShapeDtypeStruct((t, dv), F32),
                 jax.ShapeDtypeStruct((t, d), F32),
                 cv_shape]
    out_specs = [row(d), row(dqk), row(dqk), row(dv), row(dqk), row(dv), row(d), cv_spec]
    return pl.pallas_call(
        functools.partial(_stage1_kernel, decode=decode, tm=tm, dk=dqk // GLA_HEADS),
        grid=(steps,),
        in_specs=[row(d)] + [_const_spec(a.shape) for a in weights],
        out_specs=out_specs,
        out_shape=out_shape,
        scratch_shapes=[pltpu.VMEM((tm, d), BF16)],
        compiler_params=_params(("arbitrary",)),
        name="stage1_decode" if decode else "stage1_prompt",
    )(x, *weights)


def _block_ref_rows(b, w):
    c, n = b.shape
    if w >= 8:
        r = b.reshape(c // (2 * w), 2 * w, n)
        return jnp.broadcast_to(r[:, w - 1:w, :], r.shape).reshape(c, n)
    r = b.reshape(c // 8, 8, n)
    sub = lax.broadcasted_iota(jnp.int32, r.shape, 1)
    pick = lambda j: jnp.broadcast_to(r[:, j:j + 1, :], r.shape)
    if w == 4:
        out = pick(3)
    elif w == 2:
        out = jnp.where(sub < 4, pick(1), pick(5))
    else:
        out = jnp.where(sub < 2, pick(0), jnp.where(sub < 4, pick(2), jnp.where(sub < 6, pick(4), pick(6))))
    return out.reshape(c, n)


def _gla_kernel(q_ref, k_ref, b_ref, v_ref, sr_ref, gam_ref, yb_ref, sfin_ref, s_ref, *, dk, dv, chunks):
    c = pl.program_id(1)

    @pl.when(c == 0)
    def _():
        s_ref[...] = jnp.zeros_like(s_ref)

    cc = GLA_CHUNK
    row = lax.broadcasted_iota(jnp.int32, (cc, cc), 0)
    col = lax.broadcasted_iota(jnp.int32, (cc, cc), 1)
    rowk = lax.broadcasted_iota(jnp.int32, (cc, dk), 0)
    for h in range(GLA_HEADS):
        ks = slice(h * dk, (h + 1) * dk)
        vs = slice(h * dv, (h + 1) * dv)
        s_old = s_ref[h]
        for ci in range(chunks):
            rs = slice(ci * cc, (ci + 1) * cc)
            qh, kh = q_ref[rs, ks], k_ref[rs, ks]
            bh = b_ref[rs, ks] * LOG2_E
            vb = v_ref[rs, vs].astype(BF16)
            o = _dot((qh * jnp.exp2(bh)).astype(BF16), s_old.astype(BF16))
            att = jnp.where(row == col, jnp.sum(qh * kh, axis=1, keepdims=True), 0.0)
            w = cc // 2
            while w >= 1:
                diff = bh - _block_ref_rows(bh, w)
                factor = jnp.exp2(jnp.minimum(diff, -diff))
                second = (rowk // w) % 2 == 1
                ql = jnp.where(second, qh * factor, 0.0)
                kl = jnp.where(second, 0.0, kh * factor)
                a_l = _dot_nt(ql.astype(BF16), kl.astype(BF16))
                att = att + jnp.where(row // (2 * w) == col // (2 * w), a_l, 0.0)
                w //= 2
            o = o + _dot(att.astype(BF16), vb)
            b_last = bh[cc - 1:cc, :]
            kdec_t = (kh * jnp.exp2(b_last - bh)).T.astype(BF16)
            decay_col = jnp.exp2(bh.T[:, cc - 1:cc])
            s_old = decay_col * s_old + _dot(kdec_t, vb)
            o = o * lax.rsqrt(jnp.mean(o * o, axis=-1, keepdims=True) + LN_EPS) * gam_ref[...]
            yb_ref[rs, vs] = (sr_ref[rs, vs] * o).astype(BF16)
        s_ref[h] = s_old

    @pl.when(c == pl.num_programs(1) - 1)
    def _():
        sfin_ref[0] = s_ref[...]


GLA_STEP_CHUNKS = 2


def _gla_prompt(q, k, b, v, sr, gamma, *, bsz, seq_len):
    t, dqk = q.shape
    dvt = v.shape[1]
    dk, dv = dqk // GLA_HEADS, dvt // GLA_HEADS
    chunks = GLA_STEP_CHUNKS if seq_len % (GLA_STEP_CHUNKS * GLA_CHUNK) == 0 else 1
    rows = chunks * GLA_CHUNK
    nc = seq_len // rows
    row = lambda n: pl.BlockSpec((rows, n), lambda i, c: (i * nc + c, 0))
    return pl.pallas_call(
        functools.partial(_gla_kernel, dk=dk, dv=dv, chunks=chunks),
        grid=(bsz, nc),
        in_specs=[row(dqk), row(dqk), row(dqk), row(dvt), row(dvt), _const_spec(gamma.shape)],
        out_specs=[row(dvt), pl.BlockSpec((1, GLA_HEADS, dk, dv), lambda i, c: (i, 0, 0, 0))],
        out_shape=[jax.ShapeDtypeStruct((t, dvt), BF16),
                   jax.ShapeDtypeStruct((bsz, GLA_HEADS, dk, dv), F32)],
        scratch_shapes=[pltpu.VMEM((GLA_HEADS, dk, dv), F32)],
        compiler_params=_params(("arbitrary", "arbitrary")),
        name="gla_prompt",
    )(q, k, b, v, sr, gamma)


DEC_TOKENS = 16


def _gla_step_kernel(qt_ref, kt_ref, gt_ref, v_ref, sr_ref, gam_ref, s_ref, yb_ref, sout_ref, *, dk, dv):
    for j in range(DEC_TOKENS):
        for h in range(GLA_HEADS):
            ks = slice(h * dk, (h + 1) * dk)
            vs = slice(h * dv, (h + 1) * dv)
            decay = jnp.exp(gt_ref[0, ks, j:j + 1])
            s_new = decay * s_ref[j, h] + kt_ref[0, ks, j:j + 1] * v_ref[j:j + 1, vs]
            sout_ref[j, h] = s_new
            o = jnp.sum(qt_ref[0, ks, j:j + 1] * s_new, axis=0, keepdims=True)
            o = o * lax.rsqrt(jnp.mean(o * o, axis=-1, keepdims=True) + LN_EPS) * gam_ref[...]
            yb_ref[j:j + 1, vs] = (sr_ref[j:j + 1, vs] * o).astype(BF16)


def _gla_step(q, k, g, v, sr, gamma, state):
    t, dqk = q.shape
    dvt = v.shape[1]
    dk, dv = dqk // GLA_HEADS, dvt // GLA_HEADS
    steps = t // DEC_TOKENS
    tr = lambda a: a.reshape(steps, DEC_TOKENS, dqk).transpose(0, 2, 1)
    col = pl.BlockSpec((1, dqk, DEC_TOKENS), lambda i: (i, 0, 0))
    row = lambda n: pl.BlockSpec((DEC_TOKENS, n), lambda i: (i, 0))
    st = pl.BlockSpec((DEC_TOKENS, GLA_HEADS, dk, dv), lambda i: (i, 0, 0, 0))
    return pl.pallas_call(
        functools.partial(_gla_step_kernel, dk=dk, dv=dv),
        grid=(steps,),
        in_specs=[col, col, col, row(dvt), row(dvt), _const_spec(gamma.shape), st],
        out_specs=[row(dvt), st],
        out_shape=[jax.ShapeDtypeStruct((t, dvt), BF16), jax.ShapeDtypeStruct(state.shape, F32)],
        compiler_params=_params(("arbitrary",)),
        name="gla_step",
    )(tr(q), tr(k), tr(g), v, sr, gamma, state)


def _merge_kernel(ma_ref, sgb_ref, yb_ref, x_ref, wbrb_ref, wo_ref, g_ref, b_ref, x1_ref, x1t_ref, *, alpha):
    m = ma_ref[...] + sgb_ref[...] * _dot(yb_ref[...], wbrb_ref[...])
    x1 = _layernorm(alpha * x_ref[...] + _dot(m.astype(BF16), wo_ref[...]), g_ref[...], b_ref[...])
    x1_ref[...] = x1
    x1t_ref[...] = x1.T.astype(BF16)


def _merge(ma, sgb, yb, x, w, *, tm, alpha):
    t, d = x.shape
    row = pl.BlockSpec((tm, d), lambda i: (i, 0))
    weights = [w['wbrb'], w['wo'], w['ln1g'], w['ln1b']]
    return pl.pallas_call(
        functools.partial(_merge_kernel, alpha=alpha),
        grid=(t // tm,),
        in_specs=[row, row, row, row] + [_const_spec(a.shape) for a in weights],
        out_specs=[row, pl.BlockSpec((d, tm), lambda i: (0, i))],
        out_shape=[jax.ShapeDtypeStruct((t, d), F32), jax.ShapeDtypeStruct((d, t), BF16)],
        compiler_params=_params(("arbitrary",)),
        name="merge_ln1",
    )(ma, sgb, yb, x, *weights)


CAND_COUNTS = tuple(PEER_TOPK // (r1 + 1) for r1 in range(PEER_TOPK))
CAND_OFFSETS = tuple(sum(CAND_COUNTS[:r1]) for r1 in range(PEER_TOPK))
CAND_ROWS = -(-sum(CAND_COUNTS) // 8) * 8


SUBLANES = 8
KEY_TILES = PEER_NKEYS // SUBLANES
assert KEY_TILES == PEER_TOPK


def _odd_even_merge(lo, hi, r):
    step = r * 2
    if step < hi - lo:
        yield from _odd_even_merge(lo, hi, step)
        yield from _odd_even_merge(lo + r, hi, step)
        yield from [(i, i + r) for i in range(lo + r, hi - r, step)]
    else:
        yield (lo, lo + r)


def _odd_even_merge_sort(lo, hi):
    if hi - lo >= 1:
        mid = lo + (hi - lo) // 2
        yield from _odd_even_merge_sort(lo, mid)
        yield from _odd_even_merge_sort(mid + 1, hi)
        yield from _odd_even_merge(lo, hi, 1)


SORT_NET = tuple(_odd_even_merge_sort(0, KEY_TILES - 1))


def _compare_exchange(v, i, j):
    v[i], v[j] = jnp.maximum(v[i], v[j]), jnp.minimum(v[i], v[j])


def _sorted_top(s3):
    v = [s3[i] for i in range(KEY_TILES)]
    for i, j in SORT_NET:
        _compare_exchange(v, i, j)
    shift = SUBLANES // 2
    while shift >= 1:
        other = [pltpu.roll(x, shift, axis=0) for x in v]
        v = [jnp.maximum(v[i], other[PEER_TOPK - 1 - i]) for i in range(PEER_TOPK)]
        d = PEER_TOPK // 2
        while d >= 1:
            for i in range(PEER_TOPK):
                if i & d == 0:
                    _compare_exchange(v, i, i + d)
            d //= 2
        shift //= 2
    return v


def _select_top_exact(s, vals_ref):
    n = s.shape[0]
    iota = lax.broadcasted_iota(jnp.int32, s.shape, 0).astype(F32)
    rank = jnp.full(s.shape, NOT_SELECTED, F32)
    for r in range(PEER_TOPK):
        m = jnp.max(s, axis=0, keepdims=True)
        first = jnp.min(jnp.where(s == m, iota, float(n)), axis=0, keepdims=True)
        sel = iota == first
        rank = jnp.where(sel, float(r), rank)
        vals_ref[r:r + 1, :] = m
        s = jnp.where(sel, -jnp.inf, s)
    return rank


def _count(flags):
    return jnp.sum(jnp.where(flags, 1.0, 0.0), axis=0, keepdims=True)


def _top_values(s, vals_ref):
    tm = s.shape[1]
    kk = PEER_TOPK
    s3 = s.reshape(KEY_TILES, SUBLANES, tm)
    v = _sorted_top(s3)
    for r in range(kk):
        vals_ref[r:r + 1, :] = v[r][0:1, :]
    reach = jnp.sum(jnp.sum(jnp.where(s3 >= v[kk - 1][None], 1.0, 0.0), axis=0), axis=0, keepdims=True)
    equal = jnp.zeros((SUBLANES, tm), F32)
    for r in range(kk - 1):
        equal = equal + jnp.where(v[r] == v[r + 1], 1.0, 0.0)
    return v, jnp.abs(reach - kk) + equal[0:1, :]


def _choose_pairs(vals_ref, cand_ref, exact_ties):
    tm = vals_ref.shape[2]
    kk = PEER_TOPK
    cand_ref[...] = jnp.full(cand_ref.shape, -jnp.inf, F32)
    for r1 in range(kk):
        n2, off = CAND_COUNTS[r1], CAND_OFFSETS[r1]
        cand_ref[off:off + n2, :] = vals_ref[0, r1:r1 + 1, :] + vals_ref[1, 0:n2, :]
    start = cand_ref[...]
    cand = start
    iota_c = lax.broadcasted_iota(jnp.int32, cand.shape, 0).astype(F32) if exact_ties else None
    best = vals_ref[0, 0:1, :] + vals_ref[1, 0:1, :]
    z = jnp.zeros((1, tm), F32)
    for r in range(kk):
        m = jnp.max(cand, axis=0, keepdims=True)
        sel = cand == m
        if exact_ties:
            first = jnp.min(jnp.where(sel, iota_c, float(CAND_ROWS)), axis=0, keepdims=True)
            sel = iota_c == first
        z = z + jnp.exp(m - best)
        cand = jnp.where(sel, -jnp.inf, cand)
    chosen = jnp.where(start != cand, 1.0, 0.0)
    return chosen, z, jnp.abs(jnp.sum(chosen, axis=0, keepdims=True) - kk)


def _write_head(h, s1, s2, key1, match, rank2, chosen, z, vals_ref, rank2_ref, e2_ref, cnt_ref, c_ref):
    tm = s1.shape[1]
    key1 = key1.reshape(KEY_TILES, SUBLANES, tm)
    cnt = jnp.zeros(key1.shape, F32)
    for r1 in range(PEER_TOPK):
        n2, off = CAND_COUNTS[r1], CAND_OFFSETS[r1]
        count = jnp.sum(chosen[off:off + n2, :], axis=0, keepdims=True)
        cnt = cnt + jnp.where(key1 == match[r1][None], count[None], 0.0)
    rank2_ref[h] = rank2.astype(BF16)
    cnt_ref[h] = cnt.reshape(PEER_NKEYS, tm)
    c_ref[h] = jnp.exp(s1 - vals_ref[0, 0:1, :]) / z
    e2_ref[h] = jnp.exp(s2 - vals_ref[1, 0:1, :]).astype(BF16)


def _route_kernel(x1_ref, wq_ref, keys_ref, rank2_ref, e2_ref, cnt_ref, c_ref, s_ref, vals_ref, cand_ref, bad_ref):
    tm = x1_ref.shape[0]
    kk = PEER_TOPK
    outs = (rank2_ref, e2_ref, cnt_ref, c_ref)
    q = _dot(x1_ref[...].astype(BF16), wq_ref[...])
    for hp in range(2 * PEER_HEADS):
        qq = q[:, hp * LANES:(hp + 1) * LANES]
        qc = qq - jnp.mean(qq, axis=-1, keepdims=True)
        qn = (qc * lax.rsqrt(jnp.mean(qc * qc, axis=-1, keepdims=True) + LN_EPS)).astype(BF16)
        s_ref[hp] = _dot_nt(keys_ref[hp], qn)

    any_bad = jnp.zeros((1, tm), F32)
    for h in range(PEER_HEADS):
        s1, s2 = s_ref[2 * h], s_ref[2 * h + 1]
        v1, bad1 = _top_values(s1, vals_ref.at[h, 0])
        v2, bad2 = _top_values(s2, vals_ref.at[h, 1])
        s2t = s2.reshape(KEY_TILES, SUBLANES, tm)
        rank2 = jnp.zeros(s2t.shape, F32)
        for r in range(kk):
            rank2 = jnp.where(s2t < v2[r][None], float(r + 1), rank2)
        chosen, z, bad3 = _choose_pairs(vals_ref.at[h], cand_ref.at[h], False)
        _write_head(h, s1, s2, s1, v1, rank2.reshape(PEER_NKEYS, tm), chosen, z, vals_ref.at[h], *outs)
        bad = bad1 + bad2 + bad3
        bad_ref[h:h + 1, :] = bad
        any_bad = jnp.maximum(any_bad, bad)

    @pl.when(jnp.max(any_bad) > 0.0)
    def _():
        def redo(h, carry):
            @pl.when(jnp.max(bad_ref[pl.ds(h, 1), :]) > 0.0)
            def _():
                s1, s2 = s_ref[2 * h], s_ref[2 * h + 1]
                rank1 = _select_top_exact(s1, vals_ref.at[0, 0])
                rank2 = _select_top_exact(s2, vals_ref.at[0, 1])
                chosen, z, _ = _choose_pairs(vals_ref.at[0], cand_ref.at[0], True)
                ranks = [jnp.full((SUBLANES, tm), float(r), F32) for r in range(kk)]
                _write_head(h, s1, s2, rank1, ranks, rank2, chosen, z, vals_ref.at[0], *outs)
            return carry

        lax.fori_loop(0, PEER_HEADS, redo, 0)


def _route(x1, wq, keys, *, tm):
    t, d = x1.shape
    fac = pl.BlockSpec((PEER_HEADS, PEER_NKEYS, tm), lambda i: (0, 0, i))
    shape = lambda dt: jax.ShapeDtypeStruct((PEER_HEADS, PEER_NKEYS, t), dt)
    return pl.pallas_call(
        _route_kernel,
        grid=(t // tm,),
        in_specs=[pl.BlockSpec((tm, d), lambda i: (i, 0)), _const_spec(wq.shape), _const_spec(keys.shape)],
        out_specs=[fac, fac, fac, fac],
        out_shape=[shape(BF16), shape(BF16), shape(F32), shape(F32)],
        scratch_shapes=[pltpu.VMEM((2 * PEER_HEADS, PEER_NKEYS, tm), F32),
                        pltpu.VMEM((PEER_HEADS, 2, PEER_TOPK, tm), F32),
                        pltpu.VMEM((PEER_HEADS, CAND_ROWS, tm), F32),
                        pltpu.VMEM((PEER_HEADS, tm), F32)],
        compiler_params=_params(("arbitrary",)),
        name="peer_route",
    )(x1, wq, keys)


EXPERT_TILE = 2048
GROUP = 2
GATE_COLS = 256
BF16_ROWS = 16


def _experts_kernel(x1t_ref, u_ref, vt_ref, rank2_ref, e2_ref, cnt_ref, c_ref, o_ref, acc_ref, wt_ref, *, na):
    j = pl.program_id(1)
    tm = x1t_ref.shape[1]
    ch = min(GATE_COLS, tm)
    tiles = PEER_NKEYS // BF16_ROWS

    @pl.when(j == 0)
    def _():
        acc_ref[...] = jnp.zeros_like(acc_ref)

    rows = GROUP * PEER_NKEYS

    for g in range(na // GROUP):
        r0 = g * rows
        ht = _dot(u_ref[r0:r0 + rows, :], x1t_ref[...])
        for k in range(GROUP):
            row = slice(g * GROUP + k, g * GROUP + k + 1)
            for cc in range(tm // ch):
                cols = slice(cc * ch, (cc + 1) * ch)
                gate = None
                for h in range(PEER_HEADS):
                    cnt16 = jnp.broadcast_to(cnt_ref[h, row, cols], (BF16_ROWS, ch)).astype(BF16)
                    c16 = jnp.broadcast_to(c_ref[h, row, cols], (BF16_ROWS, ch)).astype(BF16)
                    r2 = rank2_ref[h, :, cols].reshape(tiles, BF16_ROWS, ch)
                    e2 = e2_ref[h, :, cols].reshape(tiles, BF16_ROWS, ch)
                    term = jnp.where(r2 < cnt16[None], e2 * c16[None], jnp.zeros((), BF16))
                    gate = term if gate is None else gate + term
                hk = _gelu(ht[k * PEER_NKEYS:(k + 1) * PEER_NKEYS, cols]).astype(BF16).reshape(tiles, BF16_ROWS, ch)
                wt_ref[r0 + k * PEER_NKEYS:r0 + (k + 1) * PEER_NKEYS, cols] = (hk * gate).reshape(PEER_NKEYS, ch)
    acc_ref[...] += _dot(vt_ref[0], wt_ref[...])

    @pl.when(j == pl.num_programs(1) - 1)
    def _():
        o_ref[...] = acc_ref[...].T


def _experts(x1t, u, vt, rank2, e2, cnt, c, *, tm, tn):
    d, t = x1t.shape
    n = u.shape[0]
    na = tn // PEER_NKEYS
    fac = pl.BlockSpec((PEER_HEADS, PEER_NKEYS, tm), lambda i, j: (0, 0, i))
    rows = pl.BlockSpec((PEER_HEADS, na, tm), lambda i, j: (0, j, i))
    return pl.pallas_call(
        functools.partial(_experts_kernel, na=na),
        grid=(t // tm, n // tn),
        in_specs=[pl.BlockSpec((d, tm), lambda i, j: (0, i)),
                  pl.BlockSpec((tn, d), lambda i, j: (j, 0)),
                  pl.BlockSpec((1, d, tn), lambda i, j: (j, 0, 0)),
                  fac, fac, rows, rows],
        out_specs=pl.BlockSpec((tm, d), lambda i, j: (i, 0)),
        out_shape=jax.ShapeDtypeStruct((t, d), F32),
        scratch_shapes=[pltpu.VMEM((d, tm), F32), pltpu.VMEM((tn, tm), BF16)],
        compiler_params=_params(("arbitrary", "arbitrary")),
        name="peer_experts",
    )(x1t, u, vt, rank2, e2, cnt, c)


def _final_kernel(x1_ref, peer_ref, p_ref, g_ref, b_ref, wpe_ref, wpg_ref, o_ref, *, alpha):
    x2 = _layernorm(alpha * x1_ref[...] + peer_ref[...], g_ref[...], b_ref[...])
    emb = _dot(p_ref[...].astype(BF16), wpe_ref[...])
    o_ref[...] = x2 + emb * jax.nn.sigmoid(_dot(x2.astype(BF16), wpg_ref[...]))


def _final(x1, peer, p, w, *, tm, alpha):
    t, d = x1.shape
    row = pl.BlockSpec((tm, d), lambda i: (i, 0))
    weights = [w['ln2g'], w['ln2b'], w['wpe'], w['wpg']]
    return pl.pallas_call(
        functools.partial(_final_kernel, alpha=alpha),
        grid=(t // tm,),
        in_specs=[row, row, pl.BlockSpec((tm, p.shape[1]), lambda i: (i, 0))]
                 + [_const_spec(a.shape) for a in weights],
        out_specs=row,
        out_shape=jax.ShapeDtypeStruct((t, d), F32),
        compiler_params=_params(("arbitrary",)),
        name="ln2_embed",
    )(x1, peer, p, *weights)


def _prep_weights(w_in, w_s, b_s, ln_v_g, ln_v_b, w_gk, b_gk, gla_norm_g, w_br_a, w_br_b, w_o, ln1_g, ln1_b,
                  peer_wq, peer_keys, peer_u, peer_v, ln2_g, ln2_b, w_pe, w_pg):
    d = w_in.shape[0]
    aw = ln_v_g.shape[0]
    dqk = w_gk.shape[1]
    dvt = w_br_b.shape[0]
    rank = w_gk.shape[0]
    sizes = (aw, aw, dqk, dqk, dvt, dvt, rank, d, d)
    offs = [0]
    for s in sizes:
        offs.append(offs[-1] + s)
    cols = lambda i: w_in[:, offs[i]:offs[i + 1]].astype(BF16)
    r2 = lambda a: a.reshape(1, -1)
    gdim = aw // A_GROUPS
    nk = peer_keys.shape[2]
    return {
        'wua': cols(0), 'wva': cols(1), 'wq': cols(2), 'wk': cols(3), 'wv': cols(4), 'wr': cols(5),
        'wgl': jnp.pad(cols(6), ((0, 0), (0, LANES - rank))),
        'wgk': jnp.pad(w_gk.astype(BF16), ((0, LANES - rank), (0, 0))),
        'bgk': r2(b_gk), 'wga': cols(7), 'wgb': cols(8),
        'ws': w_s,
        'bsx': jnp.repeat(b_s.T, gdim, axis=1),
        'ws00': r2(jnp.repeat(w_s[:, 0, 0], gdim)),
        'bs0': r2(jnp.repeat(b_s[:, 0], gdim)),
        'lng': r2(ln_v_g), 'lnb': r2(ln_v_b),
        'wbra': w_br_a.astype(BF16), 'wbrb': w_br_b.astype(BF16), 'wo': w_o.astype(BF16),
        'gamma': r2(gla_norm_g), 'ln1g': r2(ln1_g), 'ln1b': r2(ln1_b),
        'pwq': peer_wq.astype(BF16),
        'keys': peer_keys.astype(BF16).reshape(-1, nk, peer_keys.shape[3]),
        'pu': peer_u.astype(BF16),
        'pvt': peer_v.astype(BF16).reshape(-1, EXPERT_TILE, peer_v.shape[1]).transpose(0, 2, 1),
        'ln2g': r2(ln2_g), 'ln2b': r2(ln2_b), 'wpe': w_pe.astype(BF16), 'wpg': w_pg.astype(BF16),
    }


def _block(t, cap):
    best = LANES
    for m in range(LANES, cap + 1, LANES):
        if t % m == 0:
            best = m
    return best


def _layer(x, p, state, w, *, alpha):
    bsz, seq_len, d = x.shape
    t = bsz * seq_len
    x2d = x.reshape(t, d)
    decode = seq_len == 1
    assert decode or (seq_len % A_CHUNK == 0 and state is None)
    assert t % LANES == 0
    if decode:
        ma, q, k, v, b, sr, sgb, cv = _stage1(x2d, w, decode=True, tm=_block(t, 256), seq_len=1)
        yb, s_new = _gla_step(q, k, b, v, sr, w['gamma'], state)
        cv = cv.reshape(bsz, 1, d)
    else:
        ma, q, k, v, b, sr, sgb, cv = _stage1(x2d, w, decode=False, tm=_block(seq_len, 256), seq_len=seq_len)
        yb, s_new = _gla_prompt(q, k, b, v, sr, w['gamma'], bsz=bsz, seq_len=seq_len)
    x1, x1t = _merge(ma, sgb, yb, x2d, w, tm=_block(t, 512), alpha=alpha)
    rank2, e2, cnt, c = _route(x1, w['pwq'], w['keys'], tm=LANES)
    peer = _experts(x1t, w['pu'], w['pvt'], rank2, e2, cnt, c, tm=_block(t, 1024), tn=EXPERT_TILE)
    x3 = _final(x1, peer, p.reshape(t, -1), w, tm=_block(t, 512), alpha=alpha)
    return x3.reshape(bsz, seq_len, d), s_new, cv


def kernel(x_prompt, x_sample, state_gla, p_prompt, p_sample, w_in, w_s, b_s, ln_v_g, ln_v_b, w_gk, b_gk,
           gla_norm_g, w_br_a, w_br_b, w_o, ln1_g, ln1_b, peer_wq, peer_keys, peer_u, peer_v, ln2_g, ln2_b,
           w_pe, w_pg):
    depth = w_in.shape[0]
    alpha = (2.0 * depth) ** 0.25
    yp, ys = x_prompt, x_sample
    gla_p, gla_s, cv_p, cv_s = [], [], [], []
    for i in range(depth):
        w = _prep_weights(w_in[i], w_s[i], b_s[i], ln_v_g[i], ln_v_b[i], w_gk[i], b_gk[i], gla_norm_g[i],
                          w_br_a[i], w_br_b[i], w_o[i], ln1_g[i], ln1_b[i], peer_wq[i], peer_keys[i],
                          peer_u[i], peer_v[i], ln2_g[i], ln2_b[i], w_pe[i], w_pg[i])
        yp, sp, vp = _layer(yp, p_prompt[i], None, w, alpha=alpha)
        ys, ss, vs = _layer(ys, p_sample[i], state_gla[i], w, alpha=alpha)
        gla_p.append(sp)
        gla_s.append(ss)
        cv_p.append(vp)
        cv_s.append(vs)
    return (yp, ys, jnp.stack(gla_p), jnp.stack(gla_s), jnp.stack(cv_p), jnp.stack(cv_s))
```

```python
import functools
import math

import jax
import jax.numpy as jnp
from jax import lax
from jax.experimental import pallas as pl
from jax.experimental.pallas import tpu as pltpu

F32 = jnp.float32
BF16 = jnp.bfloat16

LANES = 128
A_CHUNK = 128
A_GROUPS = 8
GLA_HEADS = 4
GLA_TAU = 16.0
GLA_CHUNK = 128
PEER_HEADS = 8
PEER_NKEYS = 128
PEER_TOPK = 16
LN_EPS = 1e-5
NOT_SELECTED = 255.0
LOG2_E = math.log2(math.e)
VMEM_LIMIT = 56 * 1024 * 1024


def _dot(a, b):
    return jnp.dot(a, b, preferred_element_type=F32)


def _dot_nt(a, b):
    return lax.dot_general(a, b, (((1,), (1,)), ((), ())), preferred_element_type=F32)


def _layernorm(x, g, b):
    mu = jnp.mean(x, axis=-1, keepdims=True)
    xc = x - mu
    var = jnp.mean(xc * xc, axis=-1, keepdims=True)
    return xc * lax.rsqrt(var + LN_EPS) * g + b


def _gelu(x):
    k = -2.0 * math.sqrt(2.0 / math.pi) * math.log2(math.e)
    return x / (1.0 + jnp.exp2((x * x * (0.044715 * k) + k) * x))


def _log_sigmoid(z):
    return jnp.minimum(z, 0.0) - jnp.log1p(jnp.exp(-jnp.abs(z)))


def _const_spec(shape):
    nd = len(shape)
    return pl.BlockSpec(shape, lambda *_: (0,) * nd, pipeline_mode=pl.Buffered(1))


def _params(semantics):
    return pltpu.CompilerParams(dimension_semantics=semantics, vmem_limit_bytes=VMEM_LIMIT)


def _stage1_kernel(x_ref, wua_ref, wva_ref, wq_ref, wk_ref, wv_ref, wr_ref, wgl_ref, wgk_ref, bgk_ref,
                   wga_ref, wgb_ref, ws_ref, bs_ref, lng_ref, lnb_ref, wbra_ref,
                   ma_ref, q_ref, k_ref, v_ref, b_ref, sr_ref, sgb_ref, cv_ref, ya_ref, *, decode, tm, dk):
    xb = x_ref[...].astype(BF16)
    gu = jax.nn.gelu(_dot(xb, wua_ref[...]))
    van = _layernorm(jax.nn.gelu(_dot(xb, wva_ref[...])), lng_ref[...], lnb_ref[...])
    if decode:
        cv_ref[...] = van
        ya_ref[...] = (gu * (van * ws_ref[...] + bs_ref[...])).astype(BF16)
    else:
        cv_ref[0] = van[tm - A_CHUNK:, :]
        vb = van.astype(BF16)
        row = lax.broadcasted_iota(jnp.int32, (A_CHUNK, A_CHUNK), 0)
        col = lax.broadcasted_iota(jnp.int32, (A_CHUNK, A_CHUNK), 1)
        for g in range(A_GROUPS):
            w = jnp.where(row >= col, ws_ref[g], 0.0).astype(BF16)
            cs = slice(g * LANES, (g + 1) * LANES)
            for c in range(tm // A_CHUNK):
                rs = slice(c * A_CHUNK, (c + 1) * A_CHUNK)
                z = _dot(w, vb[rs, cs]) + bs_ref[:, cs]
                ya_ref[rs, cs] = (gu[rs, cs] * z).astype(BF16)
    ma_ref[...] = jax.nn.sigmoid(_dot(xb, wga_ref[...])) * _dot(ya_ref[...], wbra_ref[...])
    q_ref[...] = _dot(xb, wq_ref[...]) * (dk ** -0.5)
    k_ref[...] = _dot(xb, wk_ref[...])
    v_ref[...] = _dot(xb, wv_ref[...])
    sr_ref[...] = jax.nn.silu(_dot(xb, wr_ref[...]))
    sgb_ref[...] = jax.nn.sigmoid(_dot(xb, wgb_ref[...]))
    gk_low = _dot(xb, wgl_ref[...]).astype(BF16)
    log_a = _log_sigmoid(_dot(gk_low, wgk_ref[...]) + bgk_ref[...]) / GLA_TAU
    if not decode:
        pos = lax.broadcasted_iota(jnp.int32, log_a.shape, 0) % GLA_CHUNK
        shift = 1
        while shift < GLA_CHUNK:
            log_a = log_a + jnp.where(pos >= shift, pltpu.roll(log_a, shift, axis=0), 0.0)
            shift *= 2
    b_ref[...] = log_a


def _stage1(x, w, *, decode, tm, seq_len):
    t, d = x.shape
    dqk = w['wq'].shape[1]
    dv = w['wv'].shape[1]
    steps = t // tm
    row = lambda n: pl.BlockSpec((tm, n), lambda i: (i, 0))
    if decode:
        cv_shape = jax.ShapeDtypeStruct((t, d), F32)
        cv_spec = row(d)
        ws, bs = w['ws00'], w['bs0']
    else:
        per_seq = seq_len // tm
        cv_shape = jax.ShapeDtypeStruct((t // seq_len, A_CHUNK, d), F32)
        cv_spec = pl.BlockSpec((1, A_CHUNK, d), lambda i: (i // per_seq, 0, 0))
        ws, bs = w['ws'], w['bsx']
    weights = [w['wua'], w['wva'], w['wq'], w['wk'], w['wv'], w['wr'], w['wgl'], w['wgk'], w['bgk'],
               w['wga'], w['wgb'], ws, bs, w['lng'], w['lnb'], w['wbra']]
    out_shape = [jax.ShapeDtypeStruct((t, d), F32),
                 jax.ShapeDtypeStruct((t, dqk), F32),
                 jax.ShapeDtypeStruct((t, dqk), F32),
                 jax.ShapeDtypeStruct((t, dv), F32),
                 jax.ShapeDtypeStruct((t, dqk), F32),
                 jax.ShapeDtypeStruct((t, dv), F32),
                 jax.ShapeDtypeStruct((t, d), F32),
                 cv_shape]
    out_specs = [row(d), row(dqk), row(dqk), row(dv), row(dqk), row(dv), row(d), cv_spec]
    return pl.pallas_call(
        functools.partial(_stage1_kernel, decode=decode, tm=tm, dk=dqk // GLA_HEADS),
        grid=(steps,),
        in_specs=[row(d)] + [_const_spec(a.shape) for a in weights],
        out_specs=out_specs,
        out_shape=out_shape,
        scratch_shapes=[pltpu.VMEM((tm, d), BF16)],
        compiler_params=_params(("arbitrary",)),
        name="stage1_decode" if decode else "stage1_prompt",
    )(x, *weights)


def _block_ref_rows(b, w):
    c, n = b.shape
    if w >= 8:
        r = b.reshape(c // (2 * w), 2 * w, n)
        return jnp.broadcast_to(r[:, w - 1:w, :], r.shape).reshape(c, n)
    r = b.reshape(c // 8, 8, n)
    sub = lax.broadcasted_iota(jnp.int32, r.shape, 1)
    pick = lambda j: jnp.broadcast_to(r[:, j:j + 1, :], r.shape)
    if w == 4:
        out = pick(3)
    elif w == 2:
        out = jnp.where(sub < 4, pick(1), pick(5))
    else:
        out = jnp.where(sub < 2, pick(0), jnp.where(sub < 4, pick(2), jnp.where(sub < 6, pick(4), pick(6))))
    return out.reshape(c, n)


def _gla_kernel(q_ref, k_ref, b_ref, v_ref, sr_ref, gam_ref, yb_ref, sfin_ref, s_ref, *, dk, dv, chunks):
    c = pl.program_id(1)

    @pl.when(c == 0)
    def _():
        s_ref[...] = jnp.zeros_like(s_ref)

    cc = GLA_CHUNK
    row = lax.broadcasted_iota(jnp.int32, (cc, cc), 0)
    col = lax.broadcasted_iota(jnp.int32, (cc, cc), 1)
    rowk = lax.broadcasted_iota(jnp.int32, (cc, dk), 0)
    for h in range(GLA_HEADS):
        ks = slice(h * dk, (h + 1) * dk)
        vs = slice(h * dv, (h + 1) * dv)
        s_old = s_ref[h]
        for ci in range(chunks):
            rs = slice(ci * cc, (ci + 1) * cc)
            qh, kh = q_ref[rs, ks], k_ref[rs, ks]
            bh = b_ref[rs, ks] * LOG2_E
            vb = v_ref[rs, vs].astype(BF16)
            o = _dot((qh * jnp.exp2(bh)).astype(BF16), s_old.astype(BF16))
            att = jnp.where(row == col, jnp.sum(qh * kh, axis=1, keepdims=True), 0.0)
            w = cc // 2
            while w >= 1:
                diff = bh - _block_ref_rows(bh, w)
                factor = jnp.exp2(jnp.minimum(diff, -diff))
                second = (rowk // w) % 2 == 1
                ql = jnp.where(second, qh * factor, 0.0)
                kl = jnp.where(second, 0.0, kh * factor)
                a_l = _dot_nt(ql.astype(BF16), kl.astype(BF16))
                att = att + jnp.where(row // (2 * w) == col // (2 * w), a_l, 0.0)
                w //= 2
            o = o + _dot(att.astype(BF16), vb)
            b_last = bh[cc - 1:cc, :]
            kdec_t = (kh * jnp.exp2(b_last - bh)).T.astype(BF16)
            decay_col = jnp.exp2(bh.T[:, cc - 1:cc])
            s_old = decay_col * s_old + _dot(kdec_t, vb)
            o = o * lax.rsqrt(jnp.mean(o * o, axis=-1, keepdims=True) + LN_EPS) * gam_ref[...]
            yb_ref[rs, vs] = (sr_ref[rs, vs] * o).astype(BF16)
        s_ref[h] = s_old

    @pl.when(c == pl.num_programs(1) - 1)
    def _():
        sfin_ref[0] = s_ref[...]


GLA_STEP_CHUNKS = 4


def _gla_prompt(q, k, b, v, sr, gamma, *, bsz, seq_len):
    t, dqk = q.shape
    dvt = v.shape[1]
    dk, dv = dqk // GLA_HEADS, dvt // GLA_HEADS
    chunks = GLA_STEP_CHUNKS if seq_len % (GLA_STEP_CHUNKS * GLA_CHUNK) == 0 else 1
    rows = chunks * GLA_CHUNK
    nc = seq_len // rows
    row = lambda n: pl.BlockSpec((rows, n), lambda i, c: (i * nc + c, 0))
    return pl.pallas_call(
        functools.partial(_gla_kernel, dk=dk, dv=dv, chunks=chunks),
        grid=(bsz, nc),
        in_specs=[row(dqk), row(dqk), row(dqk), row(dvt), row(dvt), _const_spec(gamma.shape)],
        out_specs=[row(dvt), pl.BlockSpec((1, GLA_HEADS, dk, dv), lambda i, c: (i, 0, 0, 0))],
        out_shape=[jax.ShapeDtypeStruct((t, dvt), BF16),
                   jax.ShapeDtypeStruct((bsz, GLA_HEADS, dk, dv), F32)],
        scratch_shapes=[pltpu.VMEM((GLA_HEADS, dk, dv), F32)],
        compiler_params=_params(("arbitrary", "arbitrary")),
        name="gla_prompt",
    )(q, k, b, v, sr, gamma)


DEC_TOKENS = 16


def _gla_step_kernel(qt_ref, kt_ref, gt_ref, v_ref, sr_ref, gam_ref, s_ref, yb_ref, sout_ref, *, dk, dv):
    for j in range(DEC_TOKENS):
        for h in range(GLA_HEADS):
            ks = slice(h * dk, (h + 1) * dk)
            vs = slice(h * dv, (h + 1) * dv)
            decay = jnp.exp(gt_ref[0, ks, j:j + 1])
            s_new = decay * s_ref[j, h] + kt_ref[0, ks, j:j + 1] * v_ref[j:j + 1, vs]
            sout_ref[j, h] = s_new
            o = jnp.sum(qt_ref[0, ks, j:j + 1] * s_new, axis=0, keepdims=True)
            o = o * lax.rsqrt(jnp.mean(o * o, axis=-1, keepdims=True) + LN_EPS) * gam_ref[...]
            yb_ref[j:j + 1, vs] = (sr_ref[j:j + 1, vs] * o).astype(BF16)


def _gla_step(q, k, g, v, sr, gamma, state):
    t, dqk = q.shape
    dvt = v.shape[1]
    dk, dv = dqk // GLA_HEADS, dvt // GLA_HEADS
    steps = t // DEC_TOKENS
    tr = lambda a: a.reshape(steps, DEC_TOKENS, dqk).transpose(0, 2, 1)
    col = pl.BlockSpec((1, dqk, DEC_TOKENS), lambda i: (i, 0, 0))
    row = lambda n: pl.BlockSpec((DEC_TOKENS, n), lambda i: (i, 0))
    st = pl.BlockSpec((DEC_TOKENS, GLA_HEADS, dk, dv), lambda i: (i, 0, 0, 0))
    return pl.pallas_call(
        functools.partial(_gla_step_kernel, dk=dk, dv=dv),
        grid=(steps,),
        in_specs=[col, col, col, row(dvt), row(dvt), _const_spec(gamma.shape), st],
        out_specs=[row(dvt), st],
        out_shape=[jax.ShapeDtypeStruct((t, dvt), BF16), jax.ShapeDtypeStruct(state.shape, F32)],
        compiler_params=_params(("arbitrary",)),
        name="gla_step",
    )(tr(q), tr(k), tr(g), v, sr, gamma, state)


def _merge_kernel(ma_ref, sgb_ref, yb_ref, x_ref, wbrb_ref, wo_ref, g_ref, b_ref, x1_ref, x1t_ref, *, alpha):
    m = ma_ref[...] + sgb_ref[...] * _dot(yb_ref[...], wbrb_ref[...])
    x1 = _layernorm(alpha * x_ref[...] + _dot(m.astype(BF16), wo_ref[...]), g_ref[...], b_ref[...])
    x1_ref[...] = x1
    x1t_ref[...] = x1.T.astype(BF16)


def _merge(ma, sgb, yb, x, w, *, tm, alpha):
    t, d = x.shape
    row = pl.BlockSpec((tm, d), lambda i: (i, 0))
    weights = [w['wbrb'], w['wo'], w['ln1g'], w['ln1b']]
    return pl.pallas_call(
        functools.partial(_merge_kernel, alpha=alpha),
        grid=(t // tm,),
        in_specs=[row, row, row, row] + [_const_spec(a.shape) for a in weights],
        out_specs=[row, pl.BlockSpec((d, tm), lambda i: (0, i))],
        out_shape=[jax.ShapeDtypeStruct((t, d), F32), jax.ShapeDtypeStruct((d, t), BF16)],
        compiler_params=_params(("arbitrary",)),
        name="merge_ln1",
    )(ma, sgb, yb, x, *weights)


CAND_COUNTS = tuple(PEER_TOPK // (r1 + 1) for r1 in range(PEER_TOPK))
CAND_OFFSETS = tuple(sum(CAND_COUNTS[:r1]) for r1 in range(PEER_TOPK))
CAND_ROWS = -(-sum(CAND_COUNTS) // 8) * 8


SUBLANES = 8
KEY_TILES = PEER_NKEYS // SUBLANES
assert KEY_TILES == PEER_TOPK


def _odd_even_merge(lo, hi, r):
    step = r * 2
    if step < hi - lo:
        yield from _odd_even_merge(lo, hi, step)
        yield from _odd_even_merge(lo + r, hi, step)
        yield from [(i, i + r) for i in range(lo + r, hi - r, step)]
    else:
        yield (lo, lo + r)


def _odd_even_merge_sort(lo, hi):
    if hi - lo >= 1:
        mid = lo + (hi - lo) // 2
        yield from _odd_even_merge_sort(lo, mid)
        yield from _odd_even_merge_sort(mid + 1, hi)
        yield from _odd_even_merge(lo, hi, 1)


SORT_NET = tuple(_odd_even_merge_sort(0, KEY_TILES - 1))


def _compare_exchange(v, i, j):
    v[i], v[j] = jnp.maximum(v[i], v[j]), jnp.minimum(v[i], v[j])


def _sorted_top(s3):
    v = [s3[i] for i in range(KEY_TILES)]
    for i, j in SORT_NET:
        _compare_exchange(v, i, j)
    shift = SUBLANES // 2
    while shift >= 1:
        other = [pltpu.roll(x, shift, axis=0) for x in v]
        v = [jnp.maximum(v[i], other[PEER_TOPK - 1 - i]) for i in range(PEER_TOPK)]
        d = PEER_TOPK // 2
        while d >= 1:
            for i in range(PEER_TOPK):
                if i & d == 0:
                    _compare_exchange(v, i, i + d)
            d //= 2
        shift //= 2
    return v


def _select_top_exact(s, vals_ref):
    n = s.shape[0]
    iota = lax.broadcasted_iota(jnp.int32, s.shape, 0).astype(F32)
    rank = jnp.full(s.shape, NOT_SELECTED, F32)
    for r in range(PEER_TOPK):
        m = jnp.max(s, axis=0, keepdims=True)
        first = jnp.min(jnp.where(s == m, iota, float(n)), axis=0, keepdims=True)
        sel = iota == first
        rank = jnp.where(sel, float(r), rank)
        vals_ref[r:r + 1, :] = m
        s = jnp.where(sel, -jnp.inf, s)
    return rank


def _count(flags):
    return jnp.sum(jnp.where(flags, 1.0, 0.0), axis=0, keepdims=True)


def _top_values(s, vals_ref):
    tm = s.shape[1]
    kk = PEER_TOPK
    s3 = s.reshape(KEY_TILES, SUBLANES, tm)
    v = _sorted_top(s3)
    for r in range(kk):
        vals_ref[r:r + 1, :] = v[r][0:1, :]
    reach = jnp.sum(jnp.sum(jnp.where(s3 >= v[kk - 1][None], 1.0, 0.0), axis=0), axis=0, keepdims=True)
    equal = jnp.zeros((SUBLANES, tm), F32)
    for r in range(kk - 1):
        equal = equal + jnp.where(v[r] == v[r + 1], 1.0, 0.0)
    return v, jnp.abs(reach - kk) + equal[0:1, :]


def _choose_pairs(vals_ref, cand_ref, exact_ties):
    tm = vals_ref.shape[2]
    kk = PEER_TOPK
    cand_ref[...] = jnp.full(cand_ref.shape, -jnp.inf, F32)
    for r1 in range(kk):
        n2, off = CAND_COUNTS[r1], CAND_OFFSETS[r1]
        cand_ref[off:off + n2, :] = vals_ref[0, r1:r1 + 1, :] + vals_ref[1, 0:n2, :]
    start = cand_ref[...]
    cand = start
    iota_c = lax.broadcasted_iota(jnp.int32, cand.shape, 0).astype(F32) if exact_ties else None
    best = vals_ref[0, 0:1, :] + vals_ref[1, 0:1, :]
    z = jnp.zeros((1, tm), F32)
    for r in range(kk):
        m = jnp.max(cand, axis=0, keepdims=True)
        sel = cand == m
        if exact_ties:
            first = jnp.min(jnp.where(sel, iota_c, float(CAND_ROWS)), axis=0, keepdims=True)
            sel = iota_c == first
        z = z + jnp.exp(m - best)
        cand = jnp.where(sel, -jnp.inf, cand)
    chosen = jnp.where(start != cand, 1.0, 0.0)
    return chosen, z, jnp.abs(jnp.sum(chosen, axis=0, keepdims=True) - kk)


def _write_head(h, s1, s2, key1, match, rank2, chosen, z, vals_ref, rank2_ref, e2_ref, cnt_ref, c_ref):
    tm = s1.shape[1]
    key1 = key1.reshape(KEY_TILES, SUBLANES, tm)
    cnt = jnp.zeros(key1.shape, F32)
    for r1 in range(PEER_TOPK):
        n2, off = CAND_COUNTS[r1], CAND_OFFSETS[r1]
        count = jnp.sum(chosen[off:off + n2, :], axis=0, keepdims=True)
        cnt = cnt + jnp.where(key1 == match[r1][None], count[None], 0.0)
    rank2_ref[h] = rank2.astype(BF16)
    cnt_ref[h] = cnt.reshape(PEER_NKEYS, tm)
    c_ref[h] = jnp.exp(s1 - vals_ref[0, 0:1, :]) / z
    e2_ref[h] = jnp.exp(s2 - vals_ref[1, 0:1, :]).astype(BF16)


def _route_kernel(x1_ref, wq_ref, keys_ref, rank2_ref, e2_ref, cnt_ref, c_ref, s_ref, vals_ref, cand_ref, bad_ref):
    tm = x1_ref.shape[0]
    kk = PEER_TOPK
    outs = (rank2_ref, e2_ref, cnt_ref, c_ref)
    q = _dot(x1_ref[...].astype(BF16), wq_ref[...])
    for hp in range(2 * PEER_HEADS):
        qq = q[:, hp * LANES:(hp + 1) * LANES]
        qc = qq - jnp.mean(qq, axis=-1, keepdims=True)
        qn = (qc * lax.rsqrt(jnp.mean(qc * qc, axis=-1, keepdims=True) + LN_EPS)).astype(BF16)
        s_ref[hp] = _dot_nt(keys_ref[hp], qn)

    any_bad = jnp.zeros((1, tm), F32)
    for h in range(PEER_HEADS):
        s1, s2 = s_ref[2 * h], s_ref[2 * h + 1]
        v1, bad1 = _top_values(s1, vals_ref.at[h, 0])
        v2, bad2 = _top_values(s2, vals_ref.at[h, 1])
        s2t = s2.reshape(KEY_TILES, SUBLANES, tm)
        rank2 = jnp.zeros(s2t.shape, F32)
        for r in range(kk):
            rank2 = jnp.where(s2t < v2[r][None], float(r + 1), rank2)
        chosen, z, bad3 = _choose_pairs(vals_ref.at[h], cand_ref.at[h], False)
        _write_head(h, s1, s2, s1, v1, rank2.reshape(PEER_NKEYS, tm), chosen, z, vals_ref.at[h], *outs)
        bad = bad1 + bad2 + bad3
        bad_ref[h:h + 1, :] = bad
        any_bad = jnp.maximum(any_bad, bad)

    @pl.when(jnp.max(any_bad) > 0.0)
    def _():
        def redo(h, carry):
            @pl.when(jnp.max(bad_ref[pl.ds(h, 1), :]) > 0.0)
            def _():
                s1, s2 = s_ref[2 * h], s_ref[2 * h + 1]
                rank1 = _select_top_exact(s1, vals_ref.at[0, 0])
                rank2 = _select_top_exact(s2, vals_ref.at[0, 1])
                chosen, z, _ = _choose_pairs(vals_ref.at[0], cand_ref.at[0], True)
                ranks = [jnp.full((SUBLANES, tm), float(r), F32) for r in range(kk)]
                _write_head(h, s1, s2, rank1, ranks, rank2, chosen, z, vals_ref.at[0], *outs)
            return carry

        lax.fori_loop(0, PEER_HEADS, redo, 0)


def _route(x1, wq, keys, *, tm):
    t, d = x1.shape
    fac = pl.BlockSpec((PEER_HEADS, PEER_NKEYS, tm), lambda i: (0, 0, i))
    shape = lambda dt: jax.ShapeDtypeStruct((PEER_HEADS, PEER_NKEYS, t), dt)
    return pl.pallas_call(
        _route_kernel,
        grid=(t // tm,),
        in_specs=[pl.BlockSpec((tm, d), lambda i: (i, 0)), _const_spec(wq.shape), _const_spec(keys.shape)],
        out_specs=[fac, fac, fac, fac],
        out_shape=[shape(BF16), shape(BF16), shape(F32), shape(F32)],
        scratch_shapes=[pltpu.VMEM((2 * PEER_HEADS, PEER_NKEYS, tm), F32),
                        pltpu.VMEM((PEER_HEADS, 2, PEER_TOPK, tm), F32),
                        pltpu.VMEM((PEER_HEADS, CAND_ROWS, tm), F32),
                        pltpu.VMEM((PEER_HEADS, tm), F32)],
        compiler_params=_params(("arbitrary",)),
        name="peer_route",
    )(x1, wq, keys)


EXPERT_TILE = 2048
GROUP = 2
GATE_COLS = 256
BF16_ROWS = 16


def _experts_kernel(x1t_ref, u_ref, vt_ref, rank2_ref, e2_ref, cnt_ref, c_ref, o_ref, acc_ref, wt_ref, *, na):
    j = pl.program_id(1)
    tm = x1t_ref.shape[1]
    ch = min(GATE_COLS, tm)
    tiles = PEER_NKEYS // BF16_ROWS

    @pl.when(j == 0)
    def _():
        acc_ref[...] = jnp.zeros_like(acc_ref)

    rows = GROUP * PEER_NKEYS

    for g in range(na // GROUP):
        r0 = g * rows
        ht = _dot(u_ref[r0:r0 + rows, :], x1t_ref[...])
        for k in range(GROUP):
            row = slice(g * GROUP + k, g * GROUP + k + 1)
            for cc in range(tm // ch):
                cols = slice(cc * ch, (cc + 1) * ch)
                gate = None
                for h in range(PEER_HEADS):
                    cnt16 = jnp.broadcast_to(cnt_ref[h, row, cols], (BF16_ROWS, ch)).astype(BF16)
                    c16 = jnp.broadcast_to(c_ref[h, row, cols], (BF16_ROWS, ch)).astype(BF16)
                    r2 = rank2_ref[h, :, cols].reshape(tiles, BF16_ROWS, ch)
                    e2 = e2_ref[h, :, cols].reshape(tiles, BF16_ROWS, ch)
                    term = jnp.where(r2 < cnt16[None], e2 * c16[None], jnp.zeros((), BF16))
                    gate = term if gate is None else gate + term
                hk = _gelu(ht[k * PEER_NKEYS:(k + 1) * PEER_NKEYS, cols]).astype(BF16).reshape(tiles, BF16_ROWS, ch)
                wt_ref[r0 + k * PEER_NKEYS:r0 + (k + 1) * PEER_NKEYS, cols] = (hk * gate).reshape(PEER_NKEYS, ch)
    acc_ref[...] += _dot(vt_ref[0], wt_ref[...])

    @pl.when(j == pl.num_programs(1) - 1)
    def _():
        o_ref[...] = acc_ref[...].T


def _experts(x1t, u, vt, rank2, e2, cnt, c, *, tm, tn):
    d, t = x1t.shape
    n = u.shape[0]
    na = tn // PEER_NKEYS
    fac = pl.BlockSpec((PEER_HEADS, PEER_NKEYS, tm), lambda i, j: (0, 0, i))
    rows = pl.BlockSpec((PEER_HEADS, na, tm), lambda i, j: (0, j, i))
    return pl.pallas_call(
        functools.partial(_experts_kernel, na=na),
        grid=(t // tm, n // tn),
        in_specs=[pl.BlockSpec((d, tm), lambda i, j: (0, i)),
                  pl.BlockSpec((tn, d), lambda i, j: (j, 0)),
                  pl.BlockSpec((1, d, tn), lambda i, j: (j, 0, 0)),
                  fac, fac, rows, rows],
        out_specs=pl.BlockSpec((tm, d), lambda i, j: (i, 0)),
        out_shape=jax.ShapeDtypeStruct((t, d), F32),
        scratch_shapes=[pltpu.VMEM((d, tm), F32), pltpu.VMEM((tn, tm), BF16)],
        compiler_params=_params(("arbitrary", "arbitrary")),
        name="peer_experts",
    )(x1t, u, vt, rank2, e2, cnt, c)


def _final_kernel(x1_ref, peer_ref, p_ref, g_ref, b_ref, wpe_ref, wpg_ref, o_ref, *, alpha):
    x2 = _layernorm(alpha * x1_ref[...] + peer_ref[...], g_ref[...], b_ref[...])
    emb = _dot(p_ref[...].astype(BF16), wpe_ref[...])
    o_ref[...] = x2 + emb * jax.nn.sigmoid(_dot(x2.astype(BF16), wpg_ref[...]))


def _final(x1, peer, p, w, *, tm, alpha):
    t, d = x1.shape
    row = pl.BlockSpec((tm, d), lambda i: (i, 0))
    weights = [w['ln2g'], w['ln2b'], w['wpe'], w['wpg']]
    return pl.pallas_call(
        functools.partial(_final_kernel, alpha=alpha),
        grid=(t // tm,),
        in_specs=[row, row, pl.BlockSpec((tm, p.shape[1]), lambda i: (i, 0))]
                 + [_const_spec(a.shape) for a in weights],
        out_specs=row,
        out_shape=jax.ShapeDtypeStruct((t, d), F32),
        compiler_params=_params(("arbitrary",)),
        name="ln2_embed",
    )(x1, peer, p, *weights)


def _prep_weights(w_in, w_s, b_s, ln_v_g, ln_v_b, w_gk, b_gk, gla_norm_g, w_br_a, w_br_b, w_o, ln1_g, ln1_b,
                  peer_wq, peer_keys, peer_u, peer_v, ln2_g, ln2_b, w_pe, w_pg):
    d = w_in.shape[0]
    aw = ln_v_g.shape[0]
    dqk = w_gk.shape[1]
    dvt = w_br_b.shape[0]
    rank = w_gk.shape[0]
    sizes = (aw, aw, dqk, dqk, dvt, dvt, rank, d, d)
    offs = [0]
    for s in sizes:
        offs.append(offs[-1] + s)
    cols = lambda i: w_in[:, offs[i]:offs[i + 1]].astype(BF16)
    r2 = lambda a: a.reshape(1, -1)
    gdim = aw // A_GROUPS
    nk = peer_keys.shape[2]
    return {
        'wua': cols(0), 'wva': cols(1), 'wq': cols(2), 'wk': cols(3), 'wv': cols(4), 'wr': cols(5),
        'wgl': jnp.pad(cols(6), ((0, 0), (0, LANES - rank))),
        'wgk': jnp.pad(w_gk.astype(BF16), ((0, LANES - rank), (0, 0))),
        'bgk': r2(b_gk), 'wga': cols(7), 'wgb': cols(8),
        'ws': w_s,
        'bsx': jnp.repeat(b_s.T, gdim, axis=1),
        'ws00': r2(jnp.repeat(w_s[:, 0, 0], gdim)),
        'bs0': r2(jnp.repeat(b_s[:, 0], gdim)),
        'lng': r2(ln_v_g), 'lnb': r2(ln_v_b),
        'wbra': w_br_a.astype(BF16), 'wbrb': w_br_b.astype(BF16), 'wo': w_o.astype(BF16),
        'gamma': r2(gla_norm_g), 'ln1g': r2(ln1_g), 'ln1b': r2(ln1_b),
        'pwq': peer_wq.astype(BF16),
        'keys': peer_keys.astype(BF16).reshape(-1, nk, peer_keys.shape[3]),
        'pu': peer_u.astype(BF16),
        'pvt': peer_v.astype(BF16).reshape(-1, EXPERT_TILE, peer_v.shape[1]).transpose(0, 2, 1),
        'ln2g': r2(ln2_g), 'ln2b': r2(ln2_b), 'wpe': w_pe.astype(BF16), 'wpg': w_pg.astype(BF16),
    }


def _block(t, cap):
    best = LANES
    for m in range(LANES, cap + 1, LANES):
        if t % m == 0:
            best = m
    return best


def _layer(x, p, state, w, *, alpha):
    bsz, seq_len, d = x.shape
    t = bsz * seq_len
    x2d = x.reshape(t, d)
    decode = seq_len == 1
    assert decode or (seq_len % A_CHUNK == 0 and state is None)
    assert t % LANES == 0
    if decode:
        ma, q, k, v, b, sr, sgb, cv = _stage1(x2d, w, decode=True, tm=_block(t, 256), seq_len=1)
        yb, s_new = _gla_step(q, k, b, v, sr, w['gamma'], state)
        cv = cv.reshape(bsz, 1, d)
    else:
        ma, q, k, v, b, sr, sgb, cv = _stage1(x2d, w, decode=False, tm=_block(seq_len, 256), seq_len=seq_len)
        yb, s_new = _gla_prompt(q, k, b, v, sr, w['gamma'], bsz=bsz, seq_len=seq_len)
    x1, x1t = _merge(ma, sgb, yb, x2d, w, tm=_block(t, 512), alpha=alpha)
    rank2, e2, cnt, c = _route(x1, w['pwq'], w['keys'], tm=LANES)
    peer = _experts(x1t, w['pu'], w['pvt'], rank2, e2, cnt, c, tm=_block(t, 1024), tn=EXPERT_TILE)
    x3 = _final(x1, peer, p.reshape(t, -1), w, tm=_block(t, 512), alpha=alpha)
    return x3.reshape(bsz, seq_len, d), s_new, cv


def kernel(x_prompt, x_sample, state_gla, p_prompt, p_sample, w_in, w_s, b_s, ln_v_g, ln_v_b, w_gk, b_gk,
           gla_norm_g, w_br_a, w_br_b, w_o, ln1_g, ln1_b, peer_wq, peer_keys, peer_u, peer_v, ln2_g, ln2_b,
           w_pe, w_pg):
    depth = w_in.shape[0]
    alpha = (2.0 * depth) ** 0.25
    yp, ys = x_prompt, x_sample
    gla_p, gla_s, cv_p, cv_s = [], [], [], []
    for i in range(depth):
        w = _prep_weights(w_in[i], w_s[i], b_s[i], ln_v_g[i], ln_v_b[i], w_gk[i], b_gk[i], gla_norm_g[i],
                          w_br_a[i], w_br_b[i], w_o[i], ln1_g[i], ln1_b[i], peer_wq[i], peer_keys[i],
                          peer_u[i], peer_v[i], ln2_g[i], ln2_b[i], w_pe[i], w_pg[i])
        yp, sp, vp = _layer(yp, p_prompt[i], None, w, alpha=alpha)
        ys, ss, vs = _layer(ys, p_sample[i], state_gla[i], w, alpha=alpha)
        gla_p.append(sp)
        gla_s.append(ss)
        cv_p.append(vp)
        cv_s.append(vs)
    return (yp, ys, jnp.stack(gla_p), jnp.stack(gla_s), jnp.stack(cv_p), jnp.stack(cv_s))
```
